```python
import math
import jax, jax.numpy as jnp
from jax import lax
import numpy as np

D_MODEL = 1024
BATCH = 16
SEQ = 256
DEPTH = 4
DEC_BATCH = 4
DEC_SEQ = 4096
PAST_LEN = 512

GRID_W = 64
N_HEADS = 16
N_KV_HEADS = 4
HEAD_DIM = D_MODEL // N_HEADS
Q_W = N_HEADS * HEAD_DIM
KV_W = N_KV_HEADS * HEAD_DIM
AXIS_DIM = HEAD_DIM // 2
ROPE_THETA = 10000.0
Q_BLOCK = 128
D_CONV = D_MODEL // 2
CONV_K = 31
N_EXPERTS = 16
N_GROUPS = 4
EXPERTS_PER_GROUP = N_EXPERTS // N_GROUPS
TOP_K = 2
D_EXPERT = D_MODEL // 4
IN_W = Q_W + 2 * KV_W + 2 * D_CONV + 2 * D_MODEL
EPS = 1e-6

kernel_name = "hybrid_prefix_diffusion_gqa_conformer_moe_step"


def rmsnorm(x, g):
    x32 = x.astype(jnp.float32)
    y = x32 * lax.rsqrt(jnp.mean(x32 * x32, axis=-1, keepdims=True) + EPS)
    return (y * g.astype(jnp.float32)).astype(x.dtype)


def layernorm(x, g, b):
    x32 = x.astype(jnp.float32)
    mu = jnp.mean(x32, axis=-1, keepdims=True)
    var = jnp.mean(jnp.square(x32 - mu), axis=-1, keepdims=True)
    y = (x32 - mu) * lax.rsqrt(var + EPS) * g.astype(jnp.float32) + b.astype(jnp.float32)
    return y.astype(x.dtype)


def axial_rope_tables(n_tokens):
    rows = n_tokens // GRID_W
    row = jnp.repeat(jnp.arange(rows), GRID_W).astype(jnp.float32)
    col = jnp.tile(jnp.arange(GRID_W), rows).astype(jnp.float32)
    inv = 1.0 / (ROPE_THETA ** (jnp.arange(0, AXIS_DIM, 2, dtype=jnp.float32) / AXIS_DIM))
    ar = row[:, None] * inv[None, :]
    ac = col[:, None] * inv[None, :]
    cos = jnp.concatenate([jnp.cos(ar), jnp.cos(ar), jnp.cos(ac), jnp.cos(ac)], axis=-1)
    sin = jnp.concatenate([jnp.sin(ar), jnp.sin(ar), jnp.sin(ac), jnp.sin(ac)], axis=-1)
    return cos, sin


def apply_rope(x, cos, sin):
    x32 = x.astype(jnp.float32)
    a1, a2, b1, b2 = jnp.split(x32, 4, axis=-1)
    rot = jnp.concatenate([-a2, a1, -b2, b1], axis=-1)
    y = x32 * cos[None, :, None, :] + rot * sin[None, :, None, :]
    return y.astype(x.dtype)


def block_attention(q, k, v):
    b, sq, _, dh = q.shape
    g = N_HEADS // N_KV_HEADS
    nb = sq // Q_BLOCK
    qb = q.reshape(b, nb, Q_BLOCK, N_KV_HEADS, g, dh).transpose(1, 0, 2, 3, 4, 5)
    scale = 1.0 / math.sqrt(dh)

    def one_block(qblk):
        s = jnp.einsum('bqkgd,bskd->bkgqs', qblk, k).astype(jnp.float32) * scale
        p = jax.nn.softmax(s, axis=-1).astype(v.dtype)
        return jnp.einsum('bkgqs,bskd->bqkgd', p, v)

    o = lax.map(one_block, qb)
    return o.transpose(1, 0, 2, 3, 4, 5).reshape(b, sq, N_HEADS * dh)


def depthwise_conv(x, w, bias):
    pad = CONV_K // 2
    y = lax.conv_general_dilated(
        x, w[:, None, :].astype(x.dtype), window_strides=(1,), padding=[(pad, pad)],
        dimension_numbers=('NWC', 'WIO', 'NWC'), feature_group_count=x.shape[-1])
    return y + bias


def grouped_moe(u, w_router, router_bias, w_gate, w_up, w_down):
    b, s, d = u.shape
    t = u.reshape(b * s, d)
    scores = jax.nn.softmax((t @ w_router).astype(jnp.float32), axis=-1)
    sel = scores + router_bias.astype(jnp.float32)[None, :]
    grp = sel.reshape(-1, N_GROUPS, EXPERTS_PER_GROUP)
    grp_score = jnp.sum(lax.top_k(grp, 2)[0], axis=-1)
    gidx = jnp.argmax(grp_score, axis=-1)
    in_grp = (jnp.arange(N_EXPERTS) // EXPERTS_PER_GROUP)[None, :] == gidx[:, None]
    masked = jnp.where(in_grp, sel, -jnp.inf)
    _, eidx = lax.top_k(masked, TOP_K)
    wsel = jnp.take_along_axis(scores, eidx, axis=-1)
    wsel = wsel / jnp.sum(wsel, axis=-1, keepdims=True)
    combine = jnp.sum(jax.nn.one_hot(eidx, N_EXPERTS, dtype=jnp.float32) * wsel[..., None], axis=1)
    hg = jnp.einsum('td,edf->tef', t, w_gate)
    hu = jnp.einsum('td,edf->tef', t, w_up)
    h = jax.nn.silu(hg) * hu * combine.astype(t.dtype)[..., None]
    y = jnp.einsum('tef,efd->td', h, w_down)
    return y.reshape(b, s, d)


def trunk_layer(h, cvec, rope, ctx_k, ctx_v, norm1_g, w_ada, b_ada, w_in, q_norm_g, k_norm_g,
                w_attn_o, conv_w, conv_b, conv_ln_g, conv_ln_b, w_conv_o, w_out, norm2_g,
                w_router, router_bias, w_gate, w_up, w_down):
    b, s, _ = h.shape
    mod = jax.nn.silu(cvec) @ w_ada + b_ada
    sh1, sc1, g1, sh2, sc2, g2 = jnp.split(mod, 6, axis=-1)
    u = rmsnorm(h, norm1_g) * (1.0 + sc1) + sh1
    proj = u @ w_in
    q, k, v, a_glu, gates = jnp.split(
        proj, np.cumsum([Q_W, KV_W, KV_W, 2 * D_CONV]).tolist(), axis=-1)
    q = rmsnorm(q.reshape(b, s, N_HEADS, HEAD_DIM), q_norm_g)
    k = rmsnorm(k.reshape(b, s, N_KV_HEADS, HEAD_DIM), k_norm_g)
    v = v.reshape(b, s, N_KV_HEADS, HEAD_DIM)
    if rope is not None:
        q = apply_rope(q, rope[0], rope[1])
        k_att = apply_rope(k, rope[0], rope[1])
    else:
        k_att = k
    if ctx_k is not None:
        keys = jnp.concatenate([ctx_k.astype(k_att.dtype), k_att], axis=1)
        vals = jnp.concatenate([ctx_v.astype(v.dtype), v], axis=1)
    else:
        keys, vals = k_att, v
    attn_br = block_attention(q, keys, vals) @ w_attn_o
    ga, gb = jnp.split(a_glu, 2, axis=-1)
    cv = depthwise_conv(ga * jax.nn.sigmoid(gb), conv_w, conv_b)
    cv = jax.nn.silu(layernorm(cv, conv_ln_g, conv_ln_b))
    conv_br = cv @ w_conv_o
    gate_a, gate_c = jnp.split(gates, 2, axis=-1)
    merged = jax.nn.sigmoid(gate_a) * attn_br + jax.nn.sigmoid(gate_c) * conv_br
    h = h + g1 * (merged @ w_out)
    u2 = rmsnorm(h, norm2_g) * (1.0 + sc2) + sh2
    h = h + g2 * grouped_moe(u2, w_router, router_bias, w_gate, w_up, w_down)
    return h, k, v


def setup_inputs(seed: int = 0) -> dict:
    key = jax.random.key(seed)
    ks = jax.random.split(key, 32)
    f32 = jnp.float32

    def nrm(k, shape, scale):
        return jax.random.normal(k, shape, f32) * scale

    D = D_MODEL
    return {
        'x_prompt': nrm(ks[0], (BATCH, SEQ, D), 1.0),
        'x_sample': nrm(ks[1], (DEC_BATCH, DEC_SEQ, D), 1.0),
        'cache_k': nrm(ks[2], (DEC_BATCH, DEPTH, PAST_LEN, N_KV_HEADS, HEAD_DIM), 1.0),
        'cache_v': nrm(ks[3], (DEC_BATCH, DEPTH, PAST_LEN, N_KV_HEADS, HEAD_DIM), 1.0),
        'c': nrm(ks[4], (DEC_BATCH, D), 1.0),
        'c_ctx': nrm(ks[5], (D,), 1.0),
        'norm1_g': 1.0 + nrm(ks[6], (DEPTH, D), 0.02),
        'w_ada': nrm(ks[7], (DEPTH, D, 6 * D), 0.5 * D ** -0.5),
        'b_ada': nrm(ks[8], (DEPTH, 6 * D), 0.02),
        'w_in': nrm(ks[9], (DEPTH, D, IN_W), D ** -0.5),
        'q_norm_g': 1.0 + nrm(ks[10], (DEPTH, HEAD_DIM), 0.02),
        'k_norm_g': 1.0 + nrm(ks[11], (DEPTH, HEAD_DIM), 0.02),
        'w_attn_o': nrm(ks[12], (DEPTH, Q_W, D), Q_W ** -0.5),
        'conv_w': nrm(ks[13], (DEPTH, CONV_K, D_CONV), CONV_K ** -0.5),
        'conv_b': nrm(ks[14], (DEPTH, D_CONV), 0.02),
        'conv_ln_g': 1.0 + nrm(ks[15], (DEPTH, D_CONV), 0.02),
        'conv_ln_b': nrm(ks[16], (DEPTH, D_CONV), 0.02),
        'w_conv_o': nrm(ks[17], (DEPTH, D_CONV, D), D_CONV ** -0.5),
        'w_out': nrm(ks[18], (DEPTH, D, D), D ** -0.5),
        'norm2_g': 1.0 + nrm(ks[19], (DEPTH, D), 0.02),
        'w_router': nrm(ks[20], (D, N_EXPERTS), D ** -0.5),
        'router_bias': nrm(ks[21], (N_EXPERTS,), 0.01),
        'w_gate': nrm(ks[22], (DEPTH, N_EXPERTS, D, D_EXPERT), D ** -0.5),
        'w_up': nrm(ks[23], (DEPTH, N_EXPERTS, D, D_EXPERT), D ** -0.5),
        'w_down': nrm(ks[24], (DEPTH, N_EXPERTS, D_EXPERT, D), D_EXPERT ** -0.5),
        'final_norm_g': 1.0 + nrm(ks[25], (D,), 0.02),
    }


def reference(x_prompt, x_sample, cache_k, cache_v, c, c_ctx, norm1_g, w_ada, b_ada, w_in,
              q_norm_g, k_norm_g, w_attn_o, conv_w, conv_b, conv_ln_g, conv_ln_b, w_conv_o,
              w_out, norm2_g, w_router, router_bias, w_gate, w_up, w_down, final_norm_g):
    n_lat = x_sample.shape[1]
    rope = axial_rope_tables(n_lat)
    c_ctx_vec = c_ctx[None, None, :]
    c_vec = c[:, None, :]

    hp = x_prompt
    hs = x_sample
    ks_list = []
    vs_list = []
    for l in range(DEPTH):
        shared = (norm1_g[l], w_ada[l], b_ada[l], w_in[l], q_norm_g[l], k_norm_g[l],
                  w_attn_o[l], conv_w[l], conv_b[l], conv_ln_g[l], conv_ln_b[l], w_conv_o[l],
                  w_out[l], norm2_g[l], w_router, router_bias, w_gate[l], w_up[l], w_down[l])
        hp, k_ctx, v_ctx = trunk_layer(hp, c_ctx_vec, None, None, None, *shared)
        ks_list.append(k_ctx)
        vs_list.append(v_ctx)
        hs, _, _ = trunk_layer(hs, c_vec, rope, cache_k[:, l], cache_v[:, l], *shared)

    y_prompt = rmsnorm(hp, final_norm_g)
    y_sample = rmsnorm(hs, final_norm_g)
    new_k = jnp.stack(ks_list, axis=1)
    new_v = jnp.stack(vs_list, axis=1)
    return (y_prompt, y_sample, new_k, new_v)
```

```python
import functools
import math

import jax
import jax.numpy as jnp
from jax import lax
from jax.experimental import pallas as pl
from jax.experimental.pallas import tpu as pltpu

F32 = jnp.float32
BF16 = jnp.bfloat16

D_MODEL = 1024
N_HEADS = 16
N_KV_HEADS = 4
HEADS_PER_KV = N_HEADS // N_KV_HEADS
HEAD_DIM = 64
Q_W = N_HEADS * HEAD_DIM
KV_W = N_KV_HEADS * HEAD_DIM
QKV_W = Q_W + 2 * KV_W
D_CONV = 512
CONV_K = 31
CONV_PAD = CONV_K // 2
CONV_HALO = 16
N_EXPERTS = 16
N_GROUPS = 4
EXPERTS_PER_GROUP = N_EXPERTS // N_GROUPS
D_EXPERT = 256
MOE_W = N_EXPERTS * D_EXPERT
GRID_W = 64
ROPE_THETA = 10000.0
EPS = 1e-6
N_MOD = 6 * D_MODEL
ADA_ROWS = 8

TOKEN_TILE = 256
Q_TILE = 256
KEY_TILE_MAX = 512
ADA_COL_TILE = 1536
V7X_VMEM_LIMIT_BYTES = 56 * 1024 * 1024

NT_DIMS = (((1,), (1,)), ((), ()))
TN_DIMS = (((0,), (0,)), ((), ()))


def _params():
    return pltpu.CompilerParams(vmem_limit_bytes=V7X_VMEM_LIMIT_BYTES)


def _resident(shape):
    zeros = (0,) * len(shape)
    return pl.BlockSpec(shape, lambda *_: zeros, pipeline_mode=pl.Buffered(1))


def _sigmoid(x):
    return jax.nn.sigmoid(x)


def _ada_kernel(c_ref, w_ref, b_ref, o_ref):
    c = c_ref[...]
    x = (c * _sigmoid(c)).astype(BF16)
    w = w_ref[0].astype(BF16)
    o_ref[0] = jnp.dot(x, w, preferred_element_type=F32) + b_ref[0]


def _ada_call(cvecs, w_ada, b_ada):
    depth = w_ada.shape[0]
    return pl.pallas_call(
        _ada_kernel,
        grid=(depth, N_MOD // ADA_COL_TILE),
        in_specs=[
            pl.BlockSpec((ADA_ROWS, D_MODEL), lambda l, j: (0, 0)),
            pl.BlockSpec((1, D_MODEL, ADA_COL_TILE), lambda l, j: (l, 0, j)),
            pl.BlockSpec((1, 1, ADA_COL_TILE), lambda l, j: (l, 0, j)),
        ],
        out_specs=pl.BlockSpec((1, ADA_ROWS, ADA_COL_TILE), lambda l, j: (l, 0, j)),
        out_shape=jax.ShapeDtypeStruct((depth, ADA_ROWS, N_MOD), F32),
        compiler_params=_params(),
        name="ada_mod",
    )(cvecs, w_ada, b_ada.reshape(depth, 1, N_MOD))


def _cache_kernel(k_ref, v_ref, ko_ref, vo_ref):
    k = k_ref[0, 0]
    vt = v_ref[0, 0].T
    for h in range(N_KV_HEADS):
        lo = h * HEAD_DIM
        ko_ref[0, 0, h] = k[:, lo:lo + HEAD_DIM].astype(BF16)
        vo_ref[0, 0, h, 0] = vt[lo:lo + HEAD_DIM, :].astype(BF16)


def _cache_call(cache_k, cache_v):
    b, depth, past = cache_k.shape[:3]
    ck = cache_k.reshape(b, depth, past, KV_W)
    cv = cache_v.reshape(b, depth, past, KV_W)
    spec_in = pl.BlockSpec((1, 1, past, KV_W), lambda i, l: (i, l, 0, 0))
    return pl.pallas_call(
        _cache_kernel,
        grid=(b, depth),
        in_specs=[spec_in, spec_in],
        out_specs=[
            pl.BlockSpec((1, 1, N_KV_HEADS, past, HEAD_DIM), lambda i, l: (i, l, 0, 0, 0)),
            pl.BlockSpec((1, 1, N_KV_HEADS, 1, HEAD_DIM, past), lambda i, l: (i, l, 0, 0, 0, 0)),
        ],
        out_shape=[
            jax.ShapeDtypeStruct((b, depth, N_KV_HEADS, past, HEAD_DIM), BF16),
            jax.ShapeDtypeStruct((b, depth, N_KV_HEADS, 1, HEAD_DIM, past), BF16),
        ],
        compiler_params=_params(),
        name="cache_layout",
    )(ck, cv)


def _rope_rows(x, cos, sin):
    q4 = HEAD_DIM // 4
    a1, a2, b1, b2 = (x[:, i * q4:(i + 1) * q4] for i in range(4))
    rot = jnp.concatenate([-a2, a1, -b2, b1], axis=1)
    return x * cos[None] + rot * sin[None]


def _head_rmsnorm_rows(x, gain):
    ms = jnp.mean(x * x, axis=1, keepdims=True)
    return x * lax.rsqrt(ms + EPS) * gain[None]


def _in_proj_kernel(*refs, use_rope, emit_kv):
    it = iter(refs)
    h_ref, mod_ref, g1_ref, wqkv_ref, wrest_ref, qg_ref, kg_ref = (next(it) for _ in range(7))
    cos_ref = sin_ref = None
    if use_rope:
        cos_ref, sin_ref = next(it), next(it)
    q_ref, k_ref, v_ref, a_ref, gate_ref = (next(it) for _ in range(5))
    nk_ref = nv_ref = None
    if emit_kv:
        nk_ref, nv_ref = next(it), next(it)

    tm = h_ref.shape[0]
    x = h_ref[...]
    mod = mod_ref[0]
    sh1 = mod[:, 0:D_MODEL]
    sc1 = mod[:, D_MODEL:2 * D_MODEL]
    ms = jnp.mean(x * x, axis=-1, keepdims=True)
    u = (x * lax.rsqrt(ms + EPS) * g1_ref[...]) * (1.0 + sc1) + sh1
    ub = u.astype(BF16)

    q_t = lax.dot_general(wqkv_ref[0:Q_W, :], ub, NT_DIMS, preferred_element_type=F32)
    q3 = _head_rmsnorm_rows(q_t.reshape(N_HEADS, HEAD_DIM, tm), qg_ref[...])
    if use_rope:
        q3 = _rope_rows(q3, cos_ref[...], sin_ref[...])
    q_ref[0] = q3.astype(BF16)

    k_t = lax.dot_general(wqkv_ref[Q_W:Q_W + KV_W, :], ub, NT_DIMS, preferred_element_type=F32)
    k3 = _head_rmsnorm_rows(k_t.reshape(N_KV_HEADS, HEAD_DIM, tm), kg_ref[...])
    if emit_kv:
        nk_ref[...] = k3.reshape(KV_W, tm).T
    if use_rope:
        k3 = _rope_rows(k3, cos_ref[...], sin_ref[...])
    k_tok = k3.reshape(KV_W, tm).T.astype(BF16)
    for hh in range(N_KV_HEADS):
        k_ref[0, hh] = k_tok[:, hh * HEAD_DIM:(hh + 1) * HEAD_DIM]

    v_t = lax.dot_general(wqkv_ref[Q_W + KV_W:QKV_W, :], ub, NT_DIMS, preferred_element_type=F32)
    v_ref[0, :, 0] = v_t.reshape(N_KV_HEADS, HEAD_DIM, tm).astype(BF16)
    if emit_kv:
        nv_ref[...] = v_t.T

    glu = jnp.dot(ub, wrest_ref[:, 0:2 * D_CONV], preferred_element_type=F32)
    a_ref[0] = (glu[:, 0:D_CONV] * _sigmoid(glu[:, D_CONV:2 * D_CONV])).astype(BF16)
    gates = jnp.dot(ub, wrest_ref[:, 2 * D_CONV:], preferred_element_type=F32)
    gate_ref[...] = _sigmoid(gates).astype(BF16)


def _in_proj_call(h, mod, lw, rope, batch, seq, key_tile, emit_kv):
    tm = TOKEN_TILE
    nt = seq // tm
    tiles_per_key = key_tile // tm
    n_key_blocks = seq // key_tile
    use_rope = rope is not None
    tok = lambda b, t: (b * nt + t, 0)
    const2 = lambda b, t: (0, 0)
    in_specs = [
        pl.BlockSpec((tm, D_MODEL), tok),
        pl.BlockSpec((1, 1, N_MOD), lambda b, t: (b if mod.shape[0] > 1 else 0, 0, 0)),
        pl.BlockSpec((1, D_MODEL), const2),
        _resident((QKV_W, D_MODEL)),
        _resident((D_MODEL, 2 * D_CONV + 2 * D_MODEL)),
        pl.BlockSpec((HEAD_DIM, tm), const2),
        pl.BlockSpec((HEAD_DIM, tm), const2),
    ]
    args = [h, mod, lw["norm1_g"], lw["w_qkv_t"], lw["w_rest"], lw["q_gain"], lw["k_gain"]]
    if use_rope:
        in_specs += [pl.BlockSpec((HEAD_DIM, tm), lambda b, t: (0, t))] * 2
        args += list(rope)
    out_specs = [
        pl.BlockSpec((1, N_HEADS, HEAD_DIM, tm), lambda b, t: (b, 0, 0, t)),
        pl.BlockSpec((1, N_KV_HEADS, tm, HEAD_DIM), lambda b, t: (b, 0, t, 0)),
        pl.BlockSpec((1, N_KV_HEADS, 1, HEAD_DIM, tm),
                     lambda b, t: (b, 0, t // tiles_per_key, 0, t % tiles_per_key)),
        pl.BlockSpec((1, tm, D_CONV), lambda b, t: (b, t, 0)),
        pl.BlockSpec((tm, 2 * D_MODEL), tok),
    ]
    out_shape = [
        jax.ShapeDtypeStruct((batch, N_HEADS, HEAD_DIM, seq), BF16),
        jax.ShapeDtypeStruct((batch, N_KV_HEADS, seq, HEAD_DIM), BF16),
        jax.ShapeDtypeStruct((batch, N_KV_HEADS, n_key_blocks, HEAD_DIM, key_tile), BF16),
        jax.ShapeDtypeStruct((batch, seq, D_CONV), BF16),
        jax.ShapeDtypeStruct((batch * seq, 2 * D_MODEL), BF16),
    ]
    if emit_kv:
        out_specs += [pl.BlockSpec((tm, KV_W), tok)] * 2
        out_shape += [jax.ShapeDtypeStruct((batch * seq, KV_W), F32)] * 2
    return pl.pallas_call(
        functools.partial(_in_proj_kernel, use_rope=use_rope, emit_kv=emit_kv),
        grid=(batch, nt),
        in_specs=in_specs,
        out_specs=out_specs,
        out_shape=out_shape,
        compiler_params=_params(),
        name="in_proj",
    )(*args)


def _attn_kernel(*refs, n_src):
    q_ref = refs[0]
    srcs = [(refs[1 + 2 * i], refs[2 + 2 * i]) for i in range(n_src)]
    o_ref = refs[1 + 2 * n_src]
    m_ref, l_ref, acc_ref = refs[2 + 2 * n_src:]
    tq = q_ref.shape[3]

    q = jnp.concatenate([q_ref[0, j] for j in range(HEADS_PER_KV)], axis=1)
    m_ref[...] = jnp.full(m_ref.shape, -jnp.inf, F32)
    l_ref[...] = jnp.zeros(l_ref.shape, F32)
    acc_ref[...] = jnp.zeros(acc_ref.shape, F32)

    for k_ref, v_ref in srcs:
        n_blocks, _, tk = v_ref.shape[2:]

        def step(j, carry, k_ref=k_ref, v_ref=v_ref, tk=tk):
            off = pl.multiple_of(j * tk, tk)
            kb = k_ref[0, 0, pl.ds(off, tk), :]
            vb = v_ref[0, 0, j]
            s = jnp.dot(kb, q, preferred_element_type=F32)
            m_prev = m_ref[...]
            m_new = jnp.maximum(m_prev, jnp.max(s, axis=0, keepdims=True))
            alpha = jnp.exp(m_prev - m_new)
            p = jnp.exp(s - m_new)
            l_ref[...] = alpha * l_ref[...] + jnp.sum(p, axis=0, keepdims=True)
            acc_ref[...] = alpha * acc_ref[...] + jnp.dot(
                vb, p.astype(BF16), preferred_element_type=F32)
            m_ref[...] = m_new
            return carry

        if n_blocks == 1:
            step(0, 0)
        else:
            lax.fori_loop(0, n_blocks, step, 0)

    out = acc_ref[...] / l_ref[...]
    for j in range(HEADS_PER_KV):
        o_ref[0, j] = out[:, j * tq:(j + 1) * tq].astype(BF16)


def _attn_call(q, sources):
    batch, _, _, seq = q.shape
    tq = min(Q_TILE, seq)
    in_specs = [pl.BlockSpec((1, HEADS_PER_KV, HEAD_DIM, tq), lambda b, g, i: (b, g, 0, i))]
    args = [q]
    for k_all, v_all in sources:
        nk = k_all.shape[2]
        nb, _, tk = v_all.shape[2:]
        in_specs.append(pl.BlockSpec((1, 1, nk, HEAD_DIM), lambda b, g, i: (b, g, 0, 0)))
        in_specs.append(pl.BlockSpec((1, 1, nb, HEAD_DIM, tk), lambda b, g, i: (b, g, 0, 0, 0)))
        args += [k_all, v_all]
    lanes = HEADS_PER_KV * tq
    return pl.pallas_call(
        functools.partial(_attn_kernel, n_src=len(sources)),
        grid=(batch, N_KV_HEADS, seq // tq),
        in_specs=in_specs,
        out_specs=pl.BlockSpec((1, HEADS_PER_KV, HEAD_DIM, tq), lambda b, g, i: (b, g, 0, i)),
        out_shape=jax.ShapeDtypeStruct((batch, N_HEADS, HEAD_DIM, seq), BF16),
        scratch_shapes=[
            pltpu.VMEM((1, lanes), F32),
            pltpu.VMEM((1, lanes), F32),
            pltpu.VMEM((HEAD_DIM, lanes), F32),
        ],
        compiler_params=_params(),
        name="attn",
    )(*args)


def _router_rows(logits_ref, bias_ref, c_ref):
    lg = logits_ref[...]
    ex = jnp.exp(lg - jnp.max(lg, axis=0, keepdims=True))
    logits_ref[...] = ex / jnp.sum(ex, axis=0, keepdims=True)
    score = [logits_ref[e:e + 1, :] for e in range(N_EXPERTS)]
    sel = [score[e] + bias_ref[e:e + 1, :] for e in range(N_EXPERTS)]

    group_score = []
    for g in range(N_GROUPS):
        four = sel[g * EXPERTS_PER_GROUP:(g + 1) * EXPERTS_PER_GROUP]
        best = None
        for i in range(EXPERTS_PER_GROUP):
            for j in range(i + 1, EXPERTS_PER_GROUP):
                pair = four[i] + four[j]
                best = pair if best is None else jnp.maximum(best, pair)
        group_score.append(best)
    gmax = functools.reduce(jnp.maximum, group_score)

    in_group = []
    free = jnp.ones_like(gmax)
    for g in range(N_GROUPS):
        hit = jnp.where(group_score[g] == gmax, free, 0.0)
        in_group.append(hit)
        free = free - hit

    def pick(values, i):
        out = jnp.zeros_like(gmax)
        for g in range(N_GROUPS):
            out = jnp.where(in_group[g] > 0.0, values[g * EXPERTS_PER_GROUP + i], out)
        return out

    cand_sel = [pick(sel, i) for i in range(EXPERTS_PER_GROUP)]
    cand_score = [pick(score, i) for i in range(EXPERTS_PER_GROUP)]
    chosen = []
    for i in range(EXPERTS_PER_GROUP):
        rank = jnp.zeros_like(gmax)
        for j in range(EXPERTS_PER_GROUP):
            if j < i:
                rank = rank + jnp.where(cand_sel[j] >= cand_sel[i], 1.0, 0.0)
            elif j > i:
                rank = rank + jnp.where(cand_sel[j] > cand_sel[i], 1.0, 0.0)
        chosen.append(jnp.where(rank < 2.0, cand_score[i], 0.0))
    inv = 1.0 / functools.reduce(lambda a, b: a + b, chosen)
    weight = [c * inv for c in chosen]
    for g in range(N_GROUPS):
        for i in range(EXPERTS_PER_GROUP):
            e = g * EXPERTS_PER_GROUP + i
            c_ref[e:e + 1, :] = in_group[g] * weight[i]


def _post_kernel(*refs, has_halo):
    it = iter(refs)
    h_ref, o_ref, a_ref = next(it), next(it), next(it)
    ap_ref = an_ref = None
    if has_halo:
        ap_ref, an_ref = next(it), next(it)
    (gate_ref, mod_ref, wao_ref, wco_ref, wout_ref, cw_ref, cb_ref, lg_ref, lb_ref, g2_ref,
     wrh_ref, wrl_ref, rb_ref) = (next(it) for _ in range(13))
    hout_ref, u2_ref, c_ref = next(it), next(it), next(it)
    xs_ref, lgt_ref = next(it), next(it)

    tm = h_ref.shape[0]
    mod = mod_ref[0]
    g1 = mod[:, 2 * D_MODEL:3 * D_MODEL]
    sh2 = mod[:, 3 * D_MODEL:4 * D_MODEL]
    sc2 = mod[:, 4 * D_MODEL:5 * D_MODEL]

    o_t = o_ref[0].reshape(Q_W, tm)
    attn_br = lax.dot_general(o_t, wao_ref[...], TN_DIMS, preferred_element_type=F32)

    zeros_halo = jnp.zeros((CONV_HALO, D_CONV), F32)
    if has_halo:
        t = pl.program_id(1)
        last = pl.num_programs(1) - 1
        xs_ref[0:CONV_HALO, :] = jnp.where(t > 0, ap_ref[0].astype(F32), zeros_halo)
        xs_ref[CONV_HALO + tm:, :] = jnp.where(t < last, an_ref[0].astype(F32), zeros_halo)
    else:
        xs_ref[0:CONV_HALO, :] = zeros_halo
        xs_ref[CONV_HALO + tm:, :] = zeros_halo
    xs_ref[CONV_HALO:CONV_HALO + tm, :] = a_ref[0].astype(F32)
    cv = jnp.zeros((tm, D_CONV), F32) + cb_ref[...]
    for j in range(CONV_K):
        start = CONV_HALO - CONV_PAD + j
        cv = cv + xs_ref[start:start + tm, :] * cw_ref[j:j + 1, :]
    mu = jnp.mean(cv, axis=-1, keepdims=True)
    dev = cv - mu
    var = jnp.mean(dev * dev, axis=-1, keepdims=True)
    y = dev * lax.rsqrt(var + EPS) * lg_ref[...] + lb_ref[...]
    y = y * _sigmoid(y)
    conv_br = jnp.dot(y.astype(BF16), wco_ref[...], preferred_element_type=F32)

    gates = gate_ref[...].astype(F32)
    merged = gates[:, 0:D_MODEL] * attn_br + gates[:, D_MODEL:] * conv_br
    out = jnp.dot(merged.astype(BF16), wout_ref[...], preferred_element_type=F32)
    h1 = h_ref[...] + g1 * out
    hout_ref[...] = h1

    ms = jnp.mean(h1 * h1, axis=-1, keepdims=True)
    u2 = (h1 * lax.rsqrt(ms + EPS) * g2_ref[...]) * (1.0 + sc2) + sh2
    u2_hi = u2.astype(BF16)
    u2_ref[...] = u2_hi
    u2_lo = (u2 - u2_hi.astype(F32)).astype(BF16)
    lgt_ref[...] = (
        lax.dot_general(wrh_ref[...], u2_hi, NT_DIMS, preferred_element_type=F32)
        + lax.dot_general(wrh_ref[...], u2_lo, NT_DIMS, preferred_element_type=F32)
        + lax.dot_general(wrl_ref[...], u2_hi, NT_DIMS, preferred_element_type=F32))
    _router_rows(lgt_ref, rb_ref, c_ref)


def _post_call(h, o_t, a, gates, mod, lw, shared, batch, seq):
    tm = TOKEN_TILE
    nt = seq // tm
    has_halo = nt > 1
    halo_blocks_per_tile = tm // CONV_HALO
    n_halo_blocks = seq // CONV_HALO
    tok = lambda b, t: (b * nt + t, 0)
    const2 = lambda b, t: (0, 0)
    in_specs = [
        pl.BlockSpec((tm, D_MODEL), tok),
        pl.BlockSpec((1, N_HEADS, HEAD_DIM, tm), lambda b, t: (b, 0, 0, t)),
        pl.BlockSpec((1, tm, D_CONV), lambda b, t: (b, t, 0)),
    ]
    args = [h, o_t, a]
    if has_halo:
        in_specs += [
            pl.BlockSpec((1, CONV_HALO, D_CONV),
                         lambda b, t: (b, jnp.maximum(t * halo_blocks_per_tile - 1, 0), 0)),
            pl.BlockSpec((1, CONV_HALO, D_CONV),
                         lambda b, t: (b, jnp.minimum((t + 1) * halo_blocks_per_tile,
                                                      n_halo_blocks - 1), 0)),
        ]
        args += [a, a]
    in_specs += [
        pl.BlockSpec((tm, 2 * D_MODEL), tok),
        pl.BlockSpec((1, 1, N_MOD), lambda b, t: (b if mod.shape[0] > 1 else 0, 0, 0)),
        _resident((Q_W, D_MODEL)),
        _resident((D_CONV, D_MODEL)),
        _resident((D_MODEL, D_MODEL)),
        pl.BlockSpec((CONV_K, D_CONV), const2),
        pl.BlockSpec((1, D_CONV), const2),
        pl.BlockSpec((1, D_CONV), const2),
        pl.BlockSpec((1, D_CONV), const2),
        pl.BlockSpec((1, D_MODEL), const2),
        pl.BlockSpec((N_EXPERTS, D_MODEL), const2),
        pl.BlockSpec((N_EXPERTS, D_MODEL), const2),
        pl.BlockSpec((N_EXPERTS, tm), const2),
    ]
    args += [gates, mod, lw["w_attn_o"], lw["w_conv_o"], lw["w_out"], lw["conv_w"], lw["conv_b"],
             lw["conv_ln_g"], lw["conv_ln_b"], lw["norm2_g"],
             shared["w_router_hi"], shared["w_router_lo"], shared["router_bias"]]
    n_tok = batch * seq
    return pl.pallas_call(
        functools.partial(_post_kernel, has_halo=has_halo),
        grid=(batch, nt),
        in_specs=in_specs,
        out_specs=[
            pl.BlockSpec((tm, D_MODEL), tok),
            pl.BlockSpec((tm, D_MODEL), tok),
            pl.BlockSpec((N_EXPERTS, tm), lambda b, t: (0, b * nt + t)),
        ],
        out_shape=[
            jax.ShapeDtypeStruct((n_tok, D_MODEL), F32),
            jax.ShapeDtypeStruct((n_tok, D_MODEL), BF16),
            jax.ShapeDtypeStruct((N_EXPERTS, n_tok), F32),
        ],
        scratch_shapes=[
            pltpu.VMEM((tm + 2 * CONV_HALO, D_CONV), F32),
            pltpu.VMEM((N_EXPERTS, tm), F32),
        ],
        compiler_params=_params(),
        name="post",
    )(*args)


MOE_ROW_CHUNK = 1024


def _moe_kernel(h_ref, u_ref, c_ref, mod_ref, wg_ref, wu_ref, wd_ref, o_ref, mid_ref):
    u = u_ref[...]
    tm = u.shape[0]
    experts_per_chunk = MOE_ROW_CHUNK // D_EXPERT
    for r in range(MOE_W // MOE_ROW_CHUNK):
        rows = slice(r * MOE_ROW_CHUNK, (r + 1) * MOE_ROW_CHUNK)
        hg = lax.dot_general(wg_ref[rows, :], u, NT_DIMS, preferred_element_type=F32)
        hu = lax.dot_general(wu_ref[rows, :], u, NT_DIMS, preferred_element_type=F32)
        act = (hg * _sigmoid(hg)) * hu
        act = act.reshape(experts_per_chunk, D_EXPERT, tm)
        c = c_ref[r * experts_per_chunk:(r + 1) * experts_per_chunk, :]
        mid_ref[rows, :] = (act * c[:, None, :]).reshape(MOE_ROW_CHUNK, tm).astype(BF16)
    y_t = jnp.dot(wd_ref[...], mid_ref[...], preferred_element_type=F32)
    g2 = mod_ref[0][:, 5 * D_MODEL:6 * D_MODEL]
    o_ref[...] = h_ref[...] + g2 * y_t.T


def _moe_call(h, u2, comb, mod, lw, batch, seq):
    tm = TOKEN_TILE
    nt = seq // tm
    tok = lambda b, t: (b * nt + t, 0)
    return pl.pallas_call(
        _moe_kernel,
        grid=(batch, nt),
        in_specs=[
            pl.BlockSpec((tm, D_MODEL), tok),
            pl.BlockSpec((tm, D_MODEL), tok),
            pl.BlockSpec((N_EXPERTS, tm), lambda b, t: (0, b * nt + t)),
            pl.BlockSpec((1, 1, N_MOD), lambda b, t: (b if mod.shape[0] > 1 else 0, 0, 0)),
            _resident((MOE_W, D_MODEL)),
            _resident((MOE_W, D_MODEL)),
            _resident((D_MODEL, MOE_W)),
        ],
        out_specs=pl.BlockSpec((tm, D_MODEL), tok),
        out_shape=jax.ShapeDtypeStruct((batch * seq, D_MODEL), F32),
        scratch_shapes=[pltpu.VMEM((MOE_W, tm), BF16)],
        compiler_params=_params(),
        name="moe",
    )(h, u2, comb, mod, lw["w_gate_t"], lw["w_up_t"], lw["w_down_t"])


def _final_norm_kernel(h_ref, g_ref, o_ref):
    x = h_ref[...]
    ms = jnp.mean(x * x, axis=-1, keepdims=True)
    o_ref[...] = x * lax.rsqrt(ms + EPS) * g_ref[...]


def _final_norm_call(h, gain):
    n_tok = h.shape[0]
    tm = 512
    return pl.pallas_call(
        _final_norm_kernel,
        grid=(n_tok // tm,),
        in_specs=[pl.BlockSpec((tm, D_MODEL), lambda i: (i, 0)),
                  pl.BlockSpec((1, D_MODEL), lambda i: (0, 0))],
        out_specs=pl.BlockSpec((tm, D_MODEL), lambda i: (i, 0)),
        out_shape=jax.ShapeDtypeStruct((n_tok, D_MODEL), F32),
        compiler_params=_params(),
        name="final_norm",
    )(h, gain)


def _rope_tables_t(n_tokens):
    axis_dim = HEAD_DIM // 2
    rows = n_tokens // GRID_W
    row = jnp.repeat(jnp.arange(rows), GRID_W).astype(F32)
    col = jnp.tile(jnp.arange(GRID_W), rows).astype(F32)
    inv = 1.0 / (ROPE_THETA ** (jnp.arange(0, axis_dim, 2, dtype=F32) / axis_dim))
    ar = inv[:, None] * row[None, :]
    ac = inv[:, None] * col[None, :]
    cos = jnp.concatenate([jnp.cos(ar), jnp.cos(ar), jnp.cos(ac), jnp.cos(ac)], axis=0)
    sin = jnp.concatenate([jnp.sin(ar), jnp.sin(ar), jnp.sin(ac), jnp.sin(ac)], axis=0)
    return cos, sin


def _layer_weights(l, w):
    tm = TOKEN_TILE
    scale = 1.0 / math.sqrt(HEAD_DIM)
    w_in = w["w_in"][l]
    return {
        "norm1_g": w["norm1_g"][l].reshape(1, D_MODEL),
        "w_qkv_t": w_in[:, :QKV_W].T.astype(BF16),
        "w_rest": w_in[:, QKV_W:].astype(BF16),
        "q_gain": jnp.broadcast_to((w["q_norm_g"][l] * scale)[:, None], (HEAD_DIM, tm)),
        "k_gain": jnp.broadcast_to(w["k_norm_g"][l][:, None], (HEAD_DIM, tm)),
        "w_attn_o": w["w_attn_o"][l].astype(BF16),
        "w_conv_o": w["w_conv_o"][l].astype(BF16),
        "w_out": w["w_out"][l].astype(BF16),
        "conv_w": w["conv_w"][l],
        "conv_b": w["conv_b"][l].reshape(1, D_CONV),
        "conv_ln_g": w["conv_ln_g"][l].reshape(1, D_CONV),
        "conv_ln_b": w["conv_ln_b"][l].reshape(1, D_CONV),
        "norm2_g": w["norm2_g"][l].reshape(1, D_MODEL),
        "w_gate_t": w["w_gate"][l].transpose(0, 2, 1).reshape(MOE_W, D_MODEL).astype(BF16),
        "w_up_t": w["w_up"][l].transpose(0, 2, 1).reshape(MOE_W, D_MODEL).astype(BF16),
        "w_down_t": w["w_down"][l].transpose(2, 0, 1).reshape(D_MODEL, MOE_W).astype(BF16),
    }


def _trunk_layer(h, mod, lw, shared, rope, cache, batch, seq, emit_kv):
    key_tile = min(KEY_TILE_MAX, seq)
    outs = _in_proj_call(h, mod, lw, rope, batch, seq, key_tile, emit_kv)
    q, k_att, v_att, a, gates = outs[:5]
    sources = ([cache] if cache is not None else []) + [(k_att, v_att)]
    o_t = _attn_call(q, sources)
    h1, u2, comb = _post_call(h, o_t, a, gates, mod, lw, shared, batch, seq)
    h2 = _moe_call(h1, u2, comb, mod, lw, batch, seq)
    return h2, outs[5:]


def kernel(x_prompt, x_sample, cache_k, cache_v, c, c_ctx, norm1_g, w_ada, b_ada, w_in, q_norm_g,
           k_norm_g, w_attn_o, conv_w, conv_b, conv_ln_g, conv_ln_b, w_conv_o, w_out, norm2_g,
           w_router, router_bias, w_gate, w_up, w_down, final_norm_g):
    n_ctx, ctx_len, _ = x_prompt.shape
    n_dec, dec_len, _ = x_sample.shape
    depth = w_in.shape[0]
    weights = dict(norm1_g=norm1_g, w_in=w_in, q_norm_g=q_norm_g, k_norm_g=k_norm_g,
                   w_attn_o=w_attn_o, conv_w=conv_w, conv_b=conv_b, conv_ln_g=conv_ln_g,
                   conv_ln_b=conv_ln_b, w_conv_o=w_conv_o, w_out=w_out, norm2_g=norm2_g,
                   w_gate=w_gate, w_up=w_up, w_down=w_down)

    cvecs = jnp.zeros((ADA_ROWS, D_MODEL), F32).at[0].set(c_ctx).at[1:1 + n_dec].set(c)
    mod_all = _ada_call(cvecs, w_ada, b_ada)
    cache_kp, cache_vp = _cache_call(cache_k, cache_v)
    rope = _rope_tables_t(dec_len)

    w_router_t = w_router.T
    w_router_hi = w_router_t.astype(BF16)
    shared = {
        "w_router_hi": w_router_hi,
        "w_router_lo": (w_router_t - w_router_hi.astype(F32)).astype(BF16),
        "router_bias": jnp.broadcast_to(router_bias[:, None], (N_EXPERTS, TOKEN_TILE)),
    }

    hp = x_prompt.reshape(n_ctx * ctx_len, D_MODEL)
    hs = x_sample.reshape(n_dec * dec_len, D_MODEL)
    new_k, new_v = [], []
    for l in range(depth):
        lw = _layer_weights(l, weights)
        mod_p = mod_all[l, 0:1].reshape(1, 1, N_MOD)
        mod_s = mod_all[l, 1:1 + n_dec].reshape(n_dec, 1, N_MOD)
        hp, (nk, nv) = _trunk_layer(hp, mod_p, lw, shared, None, None, n_ctx, ctx_len, True)
        new_k.append(nk.reshape(n_ctx, ctx_len, N_KV_HEADS, HEAD_DIM))
        new_v.append(nv.reshape(n_ctx, ctx_len, N_KV_HEADS, HEAD_DIM))
        cache = (cache_kp[:, l], cache_vp[:, l])
        hs, _ = _trunk_layer(hs, mod_s, lw, shared, rope, cache, n_dec, dec_len, False)

    gain = final_norm_g.reshape(1, D_MODEL)
    y_prompt = _final_norm_call(hp, gain).reshape(n_ctx, ctx_len, D_MODEL)
    y_sample = _final_norm_call(hs, gain).reshape(n_dec, dec_len, D_MODEL)
    return (y_prompt, y_sample, jnp.stack(new_k, axis=1), jnp.stack(new_v, axis=1))
```

```python
import functools
import math

import jax
import jax.numpy as jnp
from jax import lax
from jax.experimental import pallas as pl
from jax.experimental.pallas import tpu as pltpu

F32 = jnp.float32
BF16 = jnp.bfloat16

D_MODEL = 1024
N_HEADS = 16
N_KV_HEADS = 4
HEADS_PER_KV = N_HEADS // N_KV_HEADS
LANES = 128
HEAD_DIM = 64
BF16_SUBLANE_TILE = 16
V_ROWS = HEAD_DIM + BF16_SUBLANE_TILE
Q_W = N_HEADS * HEAD_DIM
KV_W = N_KV_HEADS * HEAD_DIM
QKV_W = Q_W + 2 * KV_W
D_CONV = 512
CONV_K = 31
CONV_PAD = CONV_K // 2
CONV_HALO = 16
N_EXPERTS = 16
N_GROUPS = 4
EXPERTS_PER_GROUP = N_EXPERTS // N_GROUPS
D_EXPERT = 256
MOE_W = N_EXPERTS * D_EXPERT
GRID_W = 64
ROPE_THETA = 10000.0
EPS = 1e-6
N_MOD = 6 * D_MODEL
ADA_ROWS = 8

TOKEN_TILE = 256
Q_TILE = 256
KEY_TILE_MAX = 512
ADA_COL_TILE = 1536
V7X_VMEM_LIMIT_BYTES = 56 * 1024 * 1024

NT_DIMS = (((1,), (1,)), ((), ()))
TN_DIMS = (((0,), (0,)), ((), ()))


def _params():
    return pltpu.CompilerParams(vmem_limit_bytes=V7X_VMEM_LIMIT_BYTES)


def _resident(shape):
    zeros = (0,) * len(shape)
    return pl.BlockSpec(shape, lambda *_: zeros, pipeline_mode=pl.Buffered(1))


def _sigmoid(x):
    return jax.nn.sigmoid(x)


def _ada_kernel(c_ref, w_ref, b_ref, o_ref):
    c = c_ref[...]
    x = (c * _sigmoid(c)).astype(BF16)
    w = w_ref[0].astype(BF16)
    o_ref[0] = jnp.dot(x, w, preferred_element_type=F32) + b_ref[0]


def _ada_call(cvecs, w_ada, b_ada):
    depth = w_ada.shape[0]
    return pl.pallas_call(
        _ada_kernel,
        grid=(depth, N_MOD // ADA_COL_TILE),
        in_specs=[
            pl.BlockSpec((ADA_ROWS, D_MODEL), lambda l, j: (0, 0)),
            pl.BlockSpec((1, D_MODEL, ADA_COL_TILE), lambda l, j: (l, 0, j)),
            pl.BlockSpec((1, 1, ADA_COL_TILE), lambda l, j: (l, 0, j)),
        ],
        out_specs=pl.BlockSpec((1, ADA_ROWS, ADA_COL_TILE), lambda l, j: (l, 0, j)),
        out_shape=jax.ShapeDtypeStruct((depth, ADA_ROWS, N_MOD), F32),
        compiler_params=_params(),
        name="ada_mod",
    )(cvecs, w_ada, b_ada.reshape(depth, 1, N_MOD))


def _cache_kernel(k_ref, v_ref, ko_ref, vo_ref):
    k = k_ref[0, 0]
    vt = v_ref[0, 0].T
    ones = jnp.ones((V_ROWS - HEAD_DIM, vt.shape[1]), BF16)
    for h in range(N_KV_HEADS):
        lo = h * HEAD_DIM
        ko_ref[0, 0, h] = k[:, lo:lo + HEAD_DIM].astype(BF16)
        vo_ref[0, 0, h, 0, 0:HEAD_DIM, :] = vt[lo:lo + HEAD_DIM, :].astype(BF16)
        vo_ref[0, 0, h, 0, HEAD_DIM:V_ROWS, :] = ones


def _cache_call(cache_k, cache_v):
    b, depth, past = cache_k.shape[:3]
    ck = cache_k.reshape(b, depth, past, KV_W)
    cv = cache_v.reshape(b, depth, past, KV_W)
    spec_in = pl.BlockSpec((1, 1, past, KV_W), lambda i, l: (i, l, 0, 0))
    return pl.pallas_call(
        _cache_kernel,
        grid=(b, depth),
        in_specs=[spec_in, spec_in],
        out_specs=[
            pl.BlockSpec((1, 1, N_KV_HEADS, past, HEAD_DIM), lambda i, l: (i, l, 0, 0, 0)),
            pl.BlockSpec((1, 1, N_KV_HEADS, 1, V_ROWS, past), lambda i, l: (i, l, 0, 0, 0, 0)),
        ],
        out_shape=[
            jax.ShapeDtypeStruct((b, depth, N_KV_HEADS, past, HEAD_DIM), BF16),
            jax.ShapeDtypeStruct((b, depth, N_KV_HEADS, 1, V_ROWS, past), BF16),
        ],
        compiler_params=_params(),
        name="cache_layout",
    )(ck, cv)


def _rope_rows(x, cos, sin):
    q4 = HEAD_DIM // 4
    a1, a2, b1, b2 = (x[:, i * q4:(i + 1) * q4] for i in range(4))
    rot = jnp.concatenate([-a2, a1, -b2, b1], axis=1)
    return x * cos[None] + rot * sin[None]


def _head_rmsnorm_rows(x, gain):
    ms = jnp.mean(x * x, axis=1, keepdims=True)
    return x * lax.rsqrt(ms + EPS) * gain[None]


def _in_proj_kernel(*refs, use_rope, emit_kv):
    it = iter(refs)
    h_ref, mod_ref, g1_ref, wqkv_ref, wrest_ref, qg_ref, kg_ref = (next(it) for _ in range(7))
    cos_ref = sin_ref = None
    if use_rope:
        cos_ref, sin_ref = next(it), next(it)
    q_ref, k_ref, v_ref, a_ref, gate_ref = (next(it) for _ in range(5))
    nk_ref = nv_ref = None
    if emit_kv:
        nk_ref, nv_ref = next(it), next(it)

    tm = h_ref.shape[0]
    x = h_ref[...]
    mod = mod_ref[0]
    sh1 = mod[:, 0:D_MODEL]
    sc1 = mod[:, D_MODEL:2 * D_MODEL]
    ms = jnp.mean(x * x, axis=-1, keepdims=True)
    u = (x * lax.rsqrt(ms + EPS) * g1_ref[...]) * (1.0 + sc1) + sh1
    ub = u.astype(BF16)

    q_t = lax.dot_general(wqkv_ref[0:Q_W, :], ub, NT_DIMS, preferred_element_type=F32)
    q3 = _head_rmsnorm_rows(q_t.reshape(N_HEADS, HEAD_DIM, tm), qg_ref[...])
    if use_rope:
        q3 = _rope_rows(q3, cos_ref[...], sin_ref[...])
    q_ref[0] = q3.astype(BF16)

    k_t = lax.dot_general(wqkv_ref[Q_W:Q_W + KV_W, :], ub, NT_DIMS, preferred_element_type=F32)
    k3 = _head_rmsnorm_rows(k_t.reshape(N_KV_HEADS, HEAD_DIM, tm), kg_ref[...])
    if emit_kv:
        nk_ref[...] = k3.reshape(KV_W, tm).T
    if use_rope:
        k3 = _rope_rows(k3, cos_ref[...], sin_ref[...])
    k_tok = k3.reshape(KV_W, tm).T.astype(BF16)
    for hh in range(N_KV_HEADS):
        k_ref[0, hh] = k_tok[:, hh * HEAD_DIM:(hh + 1) * HEAD_DIM]

    v_t = lax.dot_general(wqkv_ref[Q_W + KV_W:QKV_W, :], ub, NT_DIMS, preferred_element_type=F32)
    v_ref[0, :, 0, 0:HEAD_DIM, :] = v_t.reshape(N_KV_HEADS, HEAD_DIM, tm).astype(BF16)
    v_ref[0, :, 0, HEAD_DIM:V_ROWS, :] = jnp.ones((N_KV_HEADS, V_ROWS - HEAD_DIM, tm), BF16)
    if emit_kv:
        nv_ref[...] = v_t.T

    glu = jnp.dot(ub, wrest_ref[:, 0:2 * D_CONV], preferred_element_type=F32)
    a_ref[0] = (glu[:, 0:D_CONV] * _sigmoid(glu[:, D_CONV:2 * D_CONV])).astype(BF16)
    gates = jnp.dot(ub, wrest_ref[:, 2 * D_CONV:], preferred_element_type=F32)
    gate_ref[...] = _sigmoid(gates).astype(BF16)


def _in_proj_call(h, mod, lw, rope, batch, seq, key_tile, emit_kv):
    tm = TOKEN_TILE
    nt = seq // tm
    tiles_per_key = key_tile // tm
    n_key_blocks = seq // key_tile
    use_rope = rope is not None
    tok = lambda b, t: (b * nt + t, 0)
    const2 = lambda b, t: (0, 0)
    in_specs = [
        pl.BlockSpec((tm, D_MODEL), tok),
        pl.BlockSpec((1, 1, N_MOD), lambda b, t: (b if mod.shape[0] > 1 else 0, 0, 0)),
        pl.BlockSpec((1, D_MODEL), const2),
        _resident((QKV_W, D_MODEL)),
        _resident((D_MODEL, 2 * D_CONV + 2 * D_MODEL)),
        pl.BlockSpec((HEAD_DIM, tm), const2),
        pl.BlockSpec((HEAD_DIM, tm), const2),
    ]
    args = [h, mod, lw["norm1_g"], lw["w_qkv_t"], lw["w_rest"], lw["q_gain"], lw["k_gain"]]
    if use_rope:
        in_specs += [pl.BlockSpec((HEAD_DIM, tm), lambda b, t: (0, t))] * 2
        args += list(rope)
    out_specs = [
        pl.BlockSpec((1, N_HEADS, HEAD_DIM, tm), lambda b, t: (b, 0, 0, t)),
        pl.BlockSpec((1, N_KV_HEADS, tm, HEAD_DIM), lambda b, t: (b, 0, t, 0)),
        pl.BlockSpec((1, N_KV_HEADS, 1, V_ROWS, tm),
                     lambda b, t: (b, 0, t // tiles_per_key, 0, t % tiles_per_key)),
        pl.BlockSpec((1, tm, D_CONV), lambda b, t: (b, t, 0)),
        pl.BlockSpec((tm, 2 * D_MODEL), tok),
    ]
    out_shape = [
        jax.ShapeDtypeStruct((batch, N_HEADS, HEAD_DIM, seq), BF16),
        jax.ShapeDtypeStruct((batch, N_KV_HEADS, seq, HEAD_DIM), BF16),
        jax.ShapeDtypeStruct((batch, N_KV_HEADS, n_key_blocks, V_ROWS, key_tile), BF16),
        jax.ShapeDtypeStruct((batch, seq, D_CONV), BF16),
        jax.ShapeDtypeStruct((batch * seq, 2 * D_MODEL), BF16),
    ]
    if emit_kv:
        out_specs += [pl.BlockSpec((tm, KV_W), tok)] * 2
        out_shape += [jax.ShapeDtypeStruct((batch * seq, KV_W), F32)] * 2
    return pl.pallas_call(
        functools.partial(_in_proj_kernel, use_rope=use_rope, emit_kv=emit_kv),
        grid=(batch, nt),
        in_specs=in_specs,
        out_specs=out_specs,
        out_shape=out_shape,
        compiler_params=_params(),
        name="in_proj",
    )(*args)


def _attn_kernel(*refs, n_src):
    q_ref = refs[0]
    srcs = [(refs[1 + 2 * i], refs[2 + 2 * i]) for i in range(n_src)]
    o_ref = refs[1 + 2 * n_src]
    tq = q_ref.shape[3]

    q = jnp.concatenate([q_ref[0, j] for j in range(HEADS_PER_KV)], axis=1)
    m = acc = None
    for k_ref, v_ref in srcs:
        n_blocks, _, tk = v_ref.shape[2:]
        for j in range(n_blocks):
            kb = k_ref[0, 0, j * tk:(j + 1) * tk, :]
            vb = v_ref[0, 0, j]
            s = jnp.dot(kb, q, preferred_element_type=F32)
            block_max = jnp.max(s, axis=0, keepdims=True)
            m_new = block_max if m is None else jnp.maximum(m, block_max)
            p = jnp.exp2(s - m_new).astype(BF16)
            pv = jnp.dot(vb, p, preferred_element_type=F32)
            acc = pv if m is None else jnp.exp2(m - m_new) * acc + pv
            m = m_new

    out = acc[0:HEAD_DIM] / acc[HEAD_DIM:HEAD_DIM + 1]
    for j in range(HEADS_PER_KV):
        o_ref[0, j] = out[:, j * tq:(j + 1) * tq].astype(BF16)


def _attn_call(q, sources):
    batch, _, _, seq = q.shape
    tq = min(Q_TILE, seq)
    in_specs = [pl.BlockSpec((1, HEADS_PER_KV, HEAD_DIM, tq), lambda b, g, i: (b, g, 0, i))]
    args = [q]
    for k_all, v_all in sources:
        nk = k_all.shape[2]
        nb, _, tk = v_all.shape[2:]
        in_specs.append(pl.BlockSpec((1, 1, nk, HEAD_DIM), lambda b, g, i: (b, g, 0, 0)))
        in_specs.append(pl.BlockSpec((1, 1, nb, V_ROWS, tk), lambda b, g, i: (b, g, 0, 0, 0)))
        args += [k_all, v_all]
    return pl.pallas_call(
        functools.partial(_attn_kernel, n_src=len(sources)),
        grid=(batch, N_KV_HEADS, seq // tq),
        in_specs=in_specs,
        out_specs=pl.BlockSpec((1, HEADS_PER_KV, HEAD_DIM, tq), lambda b, g, i: (b, g, 0, i)),
        out_shape=jax.ShapeDtypeStruct((batch, N_HEADS, HEAD_DIM, seq), BF16),
        compiler_params=_params(),
        name="attn",
    )(*args)


def _router_rows(logits_ref, bias_ref, c_ref):
    lg = logits_ref[...]
    ex = jnp.exp(lg - jnp.max(lg, axis=0, keepdims=True))
    logits_ref[...] = ex / jnp.sum(ex, axis=0, keepdims=True)
    score = [logits_ref[e:e + 1, :] for e in range(N_EXPERTS)]
    sel = [score[e] + bias_ref[e:e + 1, :] for e in range(N_EXPERTS)]

    group_score = []
    for g in range(N_GROUPS):
        four = sel[g * EXPERTS_PER_GROUP:(g + 1) * EXPERTS_PER_GROUP]
        best = None
        for i in range(EXPERTS_PER_GROUP):
            for j in range(i + 1, EXPERTS_PER_GROUP):
                pair = four[i] + four[j]
                best = pair if best is None else jnp.maximum(best, pair)
        group_score.append(best)
    gmax = functools.reduce(jnp.maximum, group_score)

    in_group = []
    free = jnp.ones_like(gmax)
    for g in range(N_GROUPS):
        hit = jnp.where(group_score[g] == gmax, free, 0.0)
        in_group.append(hit)
        free = free - hit

    def pick(values, i):
        out = jnp.zeros_like(gmax)
        for g in range(N_GROUPS):
            out = jnp.where(in_group[g] > 0.0, values[g * EXPERTS_PER_GROUP + i], out)
        return out

    cand_sel = [pick(sel, i) for i in range(EXPERTS_PER_GROUP)]
    cand_score = [pick(score, i) for i in range(EXPERTS_PER_GROUP)]
    chosen = []
    for i in range(EXPERTS_PER_GROUP):
        rank = jnp.zeros_like(gmax)
        for j in range(EXPERTS_PER_GROUP):
            if j < i:
                rank = rank + jnp.where(cand_sel[j] >= cand_sel[i], 1.0, 0.0)
            elif j > i:
                rank = rank + jnp.where(cand_sel[j] > cand_sel[i], 1.0, 0.0)
        chosen.append(jnp.where(rank < 2.0, cand_score[i], 0.0))
    inv = 1.0 / functools.reduce(lambda a, b: a + b, chosen)
    weight = [c * inv for c in chosen]
    for g in range(N_GROUPS):
        for i in range(EXPERTS_PER_GROUP):
            e = g * EXPERTS_PER_GROUP + i
            c_ref[e:e + 1, :] = in_group[g] * weight[i]


def _post_kernel(*refs, has_halo):
    it = iter(refs)
    h_ref, o_ref, a_ref = next(it), next(it), next(it)
    ap_ref = an_ref = None
    if has_halo:
        ap_ref, an_ref = next(it), next(it)
    (gate_ref, mod_ref, wao_ref, wco_ref, wout_ref, cw_ref, cb_ref, lg_ref, lb_ref, g2_ref,
     wrh_ref, wrl_ref, rb_ref) = (next(it) for _ in range(13))
    hout_ref, u2_ref, c_ref = next(it), next(it), next(it)
    xs_ref, lgt_ref = next(it), next(it)

    tm = h_ref.shape[0]
    mod = mod_ref[0]
    g1 = mod[:, 2 * D_MODEL:3 * D_MODEL]
    sh2 = mod[:, 3 * D_MODEL:4 * D_MODEL]
    sc2 = mod[:, 4 * D_MODEL:5 * D_MODEL]

    o_t = o_ref[0].reshape(Q_W, tm)
    attn_br = lax.dot_general(o_t, wao_ref[...], TN_DIMS, preferred_element_type=F32)

    zeros_halo = jnp.zeros((CONV_HALO, D_CONV), F32)
    if has_halo:
        t = pl.program_id(1)
        last = pl.num_programs(1) - 1
        before = jnp.where(t > 0, ap_ref[0].astype(F32), zeros_halo)
        after = jnp.where(t < last, an_ref[0].astype(F32), zeros_halo)
    else:
        before = after = zeros_halo
    a_cur = a_ref[0].astype(F32)
    slabs = []
    for s in range(D_CONV // LANES):
        cols = slice(s * LANES, (s + 1) * LANES)
        xs_ref[s, 0:CONV_HALO, :] = before[:, cols]
        xs_ref[s, CONV_HALO:CONV_HALO + tm, :] = a_cur[:, cols]
        xs_ref[s, CONV_HALO + tm:, :] = after[:, cols]
        acc = jnp.zeros((tm, LANES), F32) + cb_ref[:, cols]
        for j in range(CONV_K):
            start = CONV_HALO - CONV_PAD + j
            acc = acc + xs_ref[s, start:start + tm, :] * cw_ref[j:j + 1, cols]
        slabs.append(acc)
    cv = jnp.concatenate(slabs, axis=1)
    mu = jnp.mean(cv, axis=-1, keepdims=True)
    dev = cv - mu
    var = jnp.mean(dev * dev, axis=-1, keepdims=True)
    y = dev * lax.rsqrt(var + EPS) * lg_ref[...] + lb_ref[...]
    y = y * _sigmoid(y)
    conv_br = jnp.dot(y.astype(BF16), wco_ref[...], preferred_element_type=F32)

    gates = gate_ref[...].astype(F32)
    merged = gates[:, 0:D_MODEL] * attn_br + gates[:, D_MODEL:] * conv_br
    out = jnp.dot(merged.astype(BF16), wout_ref[...], preferred_element_type=F32)
    h1 = h_ref[...] + g1 * out
    hout_ref[...] = h1

    ms = jnp.mean(h1 * h1, axis=-1, keepdims=True)
    u2 = (h1 * lax.rsqrt(ms + EPS) * g2_ref[...]) * (1.0 + sc2) + sh2
    u2_hi = u2.astype(BF16)
    u2_ref[...] = u2_hi
    u2_lo = (u2 - u2_hi.astype(F32)).astype(BF16)
    lgt_ref[...] = (
        lax.dot_general(wrh_ref[...], u2_hi, NT_DIMS, preferred_element_type=F32)
        + lax.dot_general(wrh_ref[...], u2_lo, NT_DIMS, preferred_element_type=F32)
        + lax.dot_general(wrl_ref[...], u2_hi, NT_DIMS, preferred_element_type=F32))
    _router_rows(lgt_ref, rb_ref, c_ref)


def _post_call(h, o_t, a, gates, mod, lw, shared, batch, seq):
    tm = TOKEN_TILE
    nt = seq // tm
    has_halo = nt > 1
    halo_blocks_per_tile = tm // CONV_HALO
    n_halo_blocks = seq // CONV_HALO
    tok = lambda b, t: (b * nt + t, 0)
    const2 = lambda b, t: (0, 0)
    in_specs = [
        pl.BlockSpec((tm, D_MODEL), tok),
        pl.BlockSpec((1, N_HEADS, HEAD_DIM, tm), lambda b, t: (b, 0, 0, t)),
        pl.BlockSpec((1, tm, D_CONV), lambda b, t: (b, t, 0)),
    ]
    args = [h, o_t, a]
    if has_halo:
        in_specs += [
            pl.BlockSpec((1, CONV_HALO, D_CONV),
                         lambda b, t: (b, jnp.maximum(t * halo_blocks_per_tile - 1, 0), 0)),
            pl.BlockSpec((1, CONV_HALO, D_CONV),
                         lambda b, t: (b, jnp.minimum((t + 1) * halo_blocks_per_tile,
                                                      n_halo_blocks - 1), 0)),
        ]
        args += [a, a]
    in_specs += [
        pl.BlockSpec((tm, 2 * D_MODEL), tok),
        pl.BlockSpec((1, 1, N_MOD), lambda b, t: (b if mod.shape[0] > 1 else 0, 0, 0)),
        _resident((Q_W, D_MODEL)),
        _resident((D_CONV, D_MODEL)),
        _resident((D_MODEL, D_MODEL)),
        pl.BlockSpec((CONV_K, D_CONV), const2),
        pl.BlockSpec((1, D_CONV), const2),
        pl.BlockSpec((1, D_CONV), const2),
        pl.BlockSpec((1, D_CONV), const2),
        pl.BlockSpec((1, D_MODEL), const2),
        pl.BlockSpec((N_EXPERTS, D_MODEL), const2),
        pl.BlockSpec((N_EXPERTS, D_MODEL), const2),
        pl.BlockSpec((N_EXPERTS, tm), const2),
    ]
    args += [gates, mod, lw["w_attn_o"], lw["w_conv_o"], lw["w_out"], lw["conv_w"], lw["conv_b"],
             lw["conv_ln_g"], lw["conv_ln_b"], lw["norm2_g"],
             shared["w_router_hi"], shared["w_router_lo"], shared["router_bias"]]
    n_tok = batch * seq
    return pl.pallas_call(
        functools.partial(_post_kernel, has_halo=has_halo),
        grid=(batch, nt),
        in_specs=in_specs,
        out_specs=[
            pl.BlockSpec((tm, D_MODEL), tok),
            pl.BlockSpec((tm, D_MODEL), tok),
            pl.BlockSpec((N_EXPERTS, tm), lambda b, t: (0, b * nt + t)),
        ],
        out_shape=[
            jax.ShapeDtypeStruct((n_tok, D_MODEL), F32),
            jax.ShapeDtypeStruct((n_tok, D_MODEL), BF16),
            jax.ShapeDtypeStruct((N_EXPERTS, n_tok), F32),
        ],
        scratch_shapes=[
            pltpu.VMEM((D_CONV // LANES, tm + 2 * CONV_HALO, LANES), F32),
            pltpu.VMEM((N_EXPERTS, tm), F32),
        ],
        compiler_params=_params(),
        name="post",
    )(*args)


MOE_ROW_CHUNK = 1024


def _moe_kernel(h_ref, u_ref, c_ref, mod_ref, wg_ref, wu_ref, wd_ref, o_ref, mid_ref):
    u = u_ref[...]
    tm = u.shape[0]
    experts_per_chunk = MOE_ROW_CHUNK // D_EXPERT
    for r in range(MOE_W // MOE_ROW_CHUNK):
        rows = slice(r * MOE_ROW_CHUNK, (r + 1) * MOE_ROW_CHUNK)
        hg = lax.dot_general(wg_ref[rows, :], u, NT_DIMS, preferred_element_type=F32)
        hu = lax.dot_general(wu_ref[rows, :], u, NT_DIMS, preferred_element_type=F32)
        act = (hg * _sigmoid(hg)) * hu
        act = act.reshape(experts_per_chunk, D_EXPERT, tm)
        c = c_ref[r * experts_per_chunk:(r + 1) * experts_per_chunk, :]
        mid_ref[rows, :] = (act * c[:, None, :]).reshape(MOE_ROW_CHUNK, tm).astype(BF16)
    y_t = jnp.dot(wd_ref[...], mid_ref[...], preferred_element_type=F32)
    g2 = mod_ref[0][:, 5 * D_MODEL:6 * D_MODEL]
    o_ref[...] = h_ref[...] + g2 * y_t.T


def _moe_call(h, u2, comb, mod, lw, batch, seq):
    tm = TOKEN_TILE
    nt = seq // tm
    tok = lambda b, t: (b * nt + t, 0)
    return pl.pallas_call(
        _moe_kernel,
        grid=(batch, nt),
        in_specs=[
            pl.BlockSpec((tm, D_MODEL), tok),
            pl.BlockSpec((tm, D_MODEL), tok),
            pl.BlockSpec((N_EXPERTS, tm), lambda b, t: (0, b * nt + t)),
            pl.BlockSpec((1, 1, N_MOD), lambda b, t: (b if mod.shape[0] > 1 else 0, 0, 0)),
            _resident((MOE_W, D_MODEL)),
            _resident((MOE_W, D_MODEL)),
            _resident((D_MODEL, MOE_W)),
        ],
        out_specs=pl.BlockSpec((tm, D_MODEL), tok),
        out_shape=jax.ShapeDtypeStruct((batch * seq, D_MODEL), F32),
        scratch_shapes=[pltpu.VMEM((MOE_W, tm), BF16)],
        compiler_params=_params(),
        name="moe",
    )(h, u2, comb, mod, lw["w_gate_t"], lw["w_up_t"], lw["w_down_t"])


def _final_norm_kernel(h_ref, g_ref, o_ref):
    x = h_ref[...]
    ms = jnp.mean(x * x, axis=-1, keepdims=True)
    o_ref[...] = x * lax.rsqrt(ms + EPS) * g_ref[...]


def _final_norm_call(h, gain):
    n_tok = h.shape[0]
    tm = 512
    return pl.pallas_call(
        _final_norm_kernel,
        grid=(n_tok // tm,),
        in_specs=[pl.BlockSpec((tm, D_MODEL), lambda i: (i, 0)),
                  pl.BlockSpec((1, D_MODEL), lambda i: (0, 0))],
        out_specs=pl.BlockSpec((tm, D_MODEL), lambda i: (i, 0)),
        out_shape=jax.ShapeDtypeStruct((n_tok, D_MODEL), F32),
        compiler_params=_params(),
        name="final_norm",
    )(h, gain)


def _rope_tables_t(n_tokens):
    axis_dim = HEAD_DIM // 2
    rows = n_tokens // GRID_W
    row = jnp.repeat(jnp.arange(rows), GRID_W).astype(F32)
    col = jnp.tile(jnp.arange(GRID_W), rows).astype(F32)
    inv = 1.0 / (ROPE_THETA ** (jnp.arange(0, axis_dim, 2, dtype=F32) / axis_dim))
    ar = inv[:, None] * row[None, :]
    ac = inv[:, None] * col[None, :]
    cos = jnp.concatenate([jnp.cos(ar), jnp.cos(ar), jnp.cos(ac), jnp.cos(ac)], axis=0)
    sin = jnp.concatenate([jnp.sin(ar), jnp.sin(ar), jnp.sin(ac), jnp.sin(ac)], axis=0)
    return cos, sin


def _layer_weights(l, w):
    tm = TOKEN_TILE
    scale = math.log2(math.e) / math.sqrt(HEAD_DIM)
    w_in = w["w_in"][l]
    return {
        "norm1_g": w["norm1_g"][l].reshape(1, D_MODEL),
        "w_qkv_t": w_in[:, :QKV_W].T.astype(BF16),
        "w_rest": w_in[:, QKV_W:].astype(BF16),
        "q_gain": jnp.broadcast_to((w["q_norm_g"][l] * scale)[:, None], (HEAD_DIM, tm)),
        "k_gain": jnp.broadcast_to(w["k_norm_g"][l][:, None], (HEAD_DIM, tm)),
        "w_attn_o": w["w_attn_o"][l].astype(BF16),
        "w_conv_o": w["w_conv_o"][l].astype(BF16),
        "w_out": w["w_out"][l].astype(BF16),
        "conv_w": w["conv_w"][l],
        "conv_b": w["conv_b"][l].reshape(1, D_CONV),
        "conv_ln_g": w["conv_ln_g"][l].reshape(1, D_CONV),
        "conv_ln_b": w["conv_ln_b"][l].reshape(1, D_CONV),
        "norm2_g": w["norm2_g"][l].reshape(1, D_MODEL),
        "w_gate_t": w["w_gate"][l].transpose(0, 2, 1).reshape(MOE_W, D_MODEL).astype(BF16),
        "w_up_t": w["w_up"][l].transpose(0, 2, 1).reshape(MOE_W, D_MODEL).astype(BF16),
        "w_down_t": w["w_down"][l].transpose(2, 0, 1).reshape(D_MODEL, MOE_W).astype(BF16),
    }


def _trunk_layer(h, mod, lw, shared, rope, cache, batch, seq, emit_kv):
    key_tile = min(KEY_TILE_MAX, seq)
    outs = _in_proj_call(h, mod, lw, rope, batch, seq, key_tile, emit_kv)
    q, k_att, v_att, a, gates = outs[:5]
    sources = ([cache] if cache is not None else []) + [(k_att, v_att)]
    o_t = _attn_call(q, sources)
    h1, u2, comb = _post_call(h, o_t, a, gates, mod, lw, shared, batch, seq)
    h2 = _moe_call(h1, u2, comb, mod, lw, batch, seq)
    return h2, outs[5:]


def kernel(x_prompt, x_sample, cache_k, cache_v, c, c_ctx, norm1_g, w_ada, b_ada, w_in, q_norm_g,
           k_norm_g, w_attn_o, conv_w, conv_b, conv_ln_g, conv_ln_b, w_conv_o, w_out, norm2_g,
           w_router, router_bias, w_gate, w_up, w_down, final_norm_g):
    n_ctx, ctx_len, _ = x_prompt.shape
    n_dec, dec_len, _ = x_sample.shape
    depth = w_in.shape[0]
    weights = dict(norm1_g=norm1_g, w_in=w_in, q_norm_g=q_norm_g, k_norm_g=k_norm_g,
                   w_attn_o=w_attn_o, conv_w=conv_w, conv_b=conv_b, conv_ln_g=conv_ln_g,
                   conv_ln_b=conv_ln_b, w_conv_o=w_conv_o, w_out=w_out, norm2_g=norm2_g,
                   w_gate=w_gate, w_up=w_up, w_down=w_down)

    cvecs = jnp.zeros((ADA_ROWS, D_MODEL), F32).at[0].set(c_ctx).at[1:1 + n_dec].set(c)
    mod_all = _ada_call(cvecs, w_ada, b_ada)
    cache_kp, cache_vp = _cache_call(cache_k, cache_v)
    rope = _rope_tables_t(dec_len)

    w_router_t = w_router.T
    w_router_hi = w_router_t.astype(BF16)
    shared = {
        "w_router_hi": w_router_hi,
        "w_router_lo": (w_router_t - w_router_hi.astype(F32)).astype(BF16),
        "router_bias": jnp.broadcast_to(router_bias[:, None], (N_EXPERTS, TOKEN_TILE)),
    }

    hp = x_prompt.reshape(n_ctx * ctx_len, D_MODEL)
    hs = x_sample.reshape(n_dec * dec_len, D_MODEL)
    new_k, new_v = [], []
    for l in range(depth):
        lw = _layer_weights(l, weights)
        mod_p = mod_all[l, 0:1].reshape(1, 1, N_MOD)
        mod_s = mod_all[l, 1:1 + n_dec].reshape(n_dec, 1, N_MOD)
        hp, (nk, nv) = _trunk_layer(hp, mod_p, lw, shared, None, None, n_ctx, ctx_len, True)
        new_k.append(nk.reshape(n_ctx, ctx_len, N_KV_HEADS, HEAD_DIM))
        new_v.append(nv.reshape(n_ctx, ctx_len, N_KV_HEADS, HEAD_DIM))
        cache = (cache_kp[:, l], cache_vp[:, l])
        hs, _ = _trunk_layer(hs, mod_s, lw, shared, rope, cache, n_dec, dec_len, False)

    gain = final_norm_g.reshape(1, D_MODEL)
    y_prompt = _final_norm_call(hp, gain).reshape(n_ctx, ctx_len, D_MODEL)
    y_sample = _final_norm_call(hs, gain).reshape(n_dec, dec_len, D_MODEL)
    return (y_prompt, y_sample, jnp.stack(new_k, axis=1), jnp.stack(new_v, axis=1))
```

```python
import functools
import math

import jax
import jax.numpy as jnp
from jax import lax
from jax.experimental import pallas as pl
from jax.experimental.pallas import tpu as pltpu

F32 = jnp.float32
BF16 = jnp.bfloat16

D_MODEL = 1024
N_HEADS = 16
N_KV_HEADS = 4
HEADS_PER_KV = N_HEADS // N_KV_HEADS
LANES = 128
HEAD_DIM = 64
BF16_SUBLANE_TILE = 16
V_ROWS = HEAD_DIM + BF16_SUBLANE_TILE
Q_W = N_HEADS * HEAD_DIM
KV_W = N_KV_HEADS * HEAD_DIM
QKV_W = Q_W + 2 * KV_W
D_CONV = 512
CONV_K = 31
CONV_PAD = CONV_K // 2
CONV_HALO = 16
N_EXPERTS = 16
N_GROUPS = 4
EXPERTS_PER_GROUP = N_EXPERTS // N_GROUPS
D_EXPERT = 256
MOE_W = N_EXPERTS * D_EXPERT
GRID_W = 64
ROPE_THETA = 10000.0
EPS = 1e-6
N_MOD = 6 * D_MODEL
ADA_ROWS = 8

TOKEN_TILE = 256
Q_TILE = 1024
KEY_TILE_MAX = 256
ADA_COL_TILE = 1536
V7X_VMEM_LIMIT_BYTES = 56 * 1024 * 1024

NT_DIMS = (((1,), (1,)), ((), ()))
TN_DIMS = (((0,), (0,)), ((), ()))


def _params():
    return pltpu.CompilerParams(vmem_limit_bytes=V7X_VMEM_LIMIT_BYTES)


def _resident(shape):
    zeros = (0,) * len(shape)
    return pl.BlockSpec(shape, lambda *_: zeros, pipeline_mode=pl.Buffered(1))


def _sigmoid(x):
    return jax.nn.sigmoid(x)


def _ada_kernel(c_ref, w_ref, b_ref, o_ref):
    c = c_ref[...]
    x = (c * _sigmoid(c)).astype(BF16)
    w = w_ref[0].astype(BF16)
    o_ref[0] = jnp.dot(x, w, preferred_element_type=F32) + b_ref[0]


def _ada_call(cvecs, w_ada, b_ada):
    depth = w_ada.shape[0]
    return pl.pallas_call(
        _ada_kernel,
        grid=(depth, N_MOD // ADA_COL_TILE),
        in_specs=[
            pl.BlockSpec((ADA_ROWS, D_MODEL), lambda l, j: (0, 0)),
            pl.BlockSpec((1, D_MODEL, ADA_COL_TILE), lambda l, j: (l, 0, j)),
            pl.BlockSpec((1, 1, ADA_COL_TILE), lambda l, j: (l, 0, j)),
        ],
        out_specs=pl.BlockSpec((1, ADA_ROWS, ADA_COL_TILE), lambda l, j: (l, 0, j)),
        out_shape=jax.ShapeDtypeStruct((depth, ADA_ROWS, N_MOD), F32),
        compiler_params=_params(),
        name="ada_mod",
    )(cvecs, w_ada, b_ada.reshape(depth, 1, N_MOD))


def _cache_kernel(k_ref, v_ref, ko_ref, vo_ref):
    k = k_ref[0, 0]
    vt = v_ref[0, 0].T
    n_blocks, _, tk = vo_ref.shape[3:]
    ones = jnp.ones((V_ROWS - HEAD_DIM, tk), BF16)
    for h in range(N_KV_HEADS):
        lo = h * HEAD_DIM
        ko_ref[0, 0, h] = k[:, lo:lo + HEAD_DIM].astype(BF16)
        for j in range(n_blocks):
            vo_ref[0, 0, h, j, 0:HEAD_DIM, :] = (
                vt[lo:lo + HEAD_DIM, j * tk:(j + 1) * tk].astype(BF16))
            vo_ref[0, 0, h, j, HEAD_DIM:V_ROWS, :] = ones


def _cache_call(cache_k, cache_v, key_tile):
    b, depth, past = cache_k.shape[:3]
    n_blocks = past // key_tile
    ck = cache_k.reshape(b, depth, past, KV_W)
    cv = cache_v.reshape(b, depth, past, KV_W)
    spec_in = pl.BlockSpec((1, 1, past, KV_W), lambda i, l: (i, l, 0, 0))
    return pl.pallas_call(
        _cache_kernel,
        grid=(b, depth),
        in_specs=[spec_in, spec_in],
        out_specs=[
            pl.BlockSpec((1, 1, N_KV_HEADS, past, HEAD_DIM), lambda i, l: (i, l, 0, 0, 0)),
            pl.BlockSpec((1, 1, N_KV_HEADS, n_blocks, V_ROWS, key_tile),
                         lambda i, l: (i, l, 0, 0, 0, 0)),
        ],
        out_shape=[
            jax.ShapeDtypeStruct((b, depth, N_KV_HEADS, past, HEAD_DIM), BF16),
            jax.ShapeDtypeStruct((b, depth, N_KV_HEADS, n_blocks, V_ROWS, key_tile), BF16),
        ],
        compiler_params=_params(),
        name="cache_layout",
    )(ck, cv)


def _rope_rows(x, cos, sin):
    q4 = HEAD_DIM // 4
    a1, a2, b1, b2 = (x[:, i * q4:(i + 1) * q4] for i in range(4))
    rot = jnp.concatenate([-a2, a1, -b2, b1], axis=1)
    return x * cos[None] + rot * sin[None]


def _head_rmsnorm_rows(x, gain):
    ms = jnp.mean(x * x, axis=1, keepdims=True)
    return x * lax.rsqrt(ms + EPS) * gain[None]


def _in_proj_kernel(*refs, use_rope, emit_kv):
    it = iter(refs)
    h_ref, mod_ref, g1_ref, wqkv_ref, wrest_ref, qg_ref, kg_ref = (next(it) for _ in range(7))
    cos_ref = sin_ref = None
    if use_rope:
        cos_ref, sin_ref = next(it), next(it)
    q_ref, k_ref, v_ref, a_ref, gate_ref = (next(it) for _ in range(5))
    nk_ref = nv_ref = None
    if emit_kv:
        nk_ref, nv_ref = next(it), next(it)

    tm = h_ref.shape[0]
    x = h_ref[...]
    mod = mod_ref[0]
    sh1 = mod[:, 0:D_MODEL]
    sc1 = mod[:, D_MODEL:2 * D_MODEL]
    ms = jnp.mean(x * x, axis=-1, keepdims=True)
    u = (x * lax.rsqrt(ms + EPS) * g1_ref[...]) * (1.0 + sc1) + sh1
    ub = u.astype(BF16)

    q_t = lax.dot_general(wqkv_ref[0:Q_W, :], ub, NT_DIMS, preferred_element_type=F32)
    q3 = _head_rmsnorm_rows(q_t.reshape(N_HEADS, HEAD_DIM, tm), qg_ref[...])
    if use_rope:
        q3 = _rope_rows(q3, cos_ref[...], sin_ref[...])
    q_ref[0] = q3.astype(BF16)

    k_t = lax.dot_general(wqkv_ref[Q_W:Q_W + KV_W, :], ub, NT_DIMS, preferred_element_type=F32)
    k3 = _head_rmsnorm_rows(k_t.reshape(N_KV_HEADS, HEAD_DIM, tm), kg_ref[...])
    if emit_kv:
        nk_ref[...] = k3.reshape(KV_W, tm).T
    if use_rope:
        k3 = _rope_rows(k3, cos_ref[...], sin_ref[...])
    k_tok = k3.reshape(KV_W, tm).T.astype(BF16)
    for hh in range(N_KV_HEADS):
        k_ref[0, hh] = k_tok[:, hh * HEAD_DIM:(hh + 1) * HEAD_DIM]

    v_t = lax.dot_general(wqkv_ref[Q_W + KV_W:QKV_W, :], ub, NT_DIMS, preferred_element_type=F32)
    v_ref[0, :, 0, 0:HEAD_DIM, :] = v_t.reshape(N_KV_HEADS, HEAD_DIM, tm).astype(BF16)
    v_ref[0, :, 0, HEAD_DIM:V_ROWS, :] = jnp.ones((N_KV_HEADS, V_ROWS - HEAD_DIM, tm), BF16)
    if emit_kv:
        nv_ref[...] = v_t.T

    glu = jnp.dot(ub, wrest_ref[:, 0:2 * D_CONV], preferred_element_type=F32)
    a_ref[0] = (glu[:, 0:D_CONV] * _sigmoid(glu[:, D_CONV:2 * D_CONV])).astype(BF16)
    gates = jnp.dot(ub, wrest_ref[:, 2 * D_CONV:], preferred_element_type=F32)
    gate_ref[...] = _sigmoid(gates).astype(BF16)


def _in_proj_call(h, mod, lw, rope, batch, seq, key_tile, emit_kv):
    tm = TOKEN_TILE
    nt = seq // tm
    tiles_per_key = key_tile // tm
    n_key_blocks = seq // key_tile
    use_rope = rope is not None
    tok = lambda b, t: (b * nt + t, 0)
    const2 = lambda b, t: (0, 0)
    in_specs = [
        pl.BlockSpec((tm, D_MODEL), tok),
        pl.BlockSpec((1, 1, N_MOD), lambda b, t: (b if mod.shape[0] > 1 else 0, 0, 0)),
        pl.BlockSpec((1, D_MODEL), const2),
        _resident((QKV_W, D_MODEL)),
        _resident((D_MODEL, 2 * D_CONV + 2 * D_MODEL)),
        pl.BlockSpec((HEAD_DIM, tm), const2),
        pl.BlockSpec((HEAD_DIM, tm), const2),
    ]
    args = [h, mod, lw["norm1_g"], lw["w_qkv_t"], lw["w_rest"], lw["q_gain"], lw["k_gain"]]
    if use_rope:
        in_specs += [pl.BlockSpec((HEAD_DIM, tm), lambda b, t: (0, t))] * 2
        args += list(rope)
    out_specs = [
        pl.BlockSpec((1, N_HEADS, HEAD_DIM, tm), lambda b, t: (b, 0, 0, t)),
        pl.BlockSpec((1, N_KV_HEADS, tm, HEAD_DIM), lambda b, t: (b, 0, t, 0)),
        pl.BlockSpec((1, N_KV_HEADS, 1, V_ROWS, tm),
                     lambda b, t: (b, 0, t // tiles_per_key, 0, t % tiles_per_key)),
        pl.BlockSpec((1, tm, D_CONV), lambda b, t: (b, t, 0)),
        pl.BlockSpec((tm, 2 * D_MODEL), tok),
    ]
    out_shape = [
        jax.ShapeDtypeStruct((batch, N_HEADS, HEAD_DIM, seq), BF16),
        jax.ShapeDtypeStruct((batch, N_KV_HEADS, seq, HEAD_DIM), BF16),
        jax.ShapeDtypeStruct((batch, N_KV_HEADS, n_key_blocks, V_ROWS, key_tile), BF16),
        jax.ShapeDtypeStruct((batch, seq, D_CONV), BF16),
        jax.ShapeDtypeStruct((batch * seq, 2 * D_MODEL), BF16),
    ]
    if emit_kv:
        out_specs += [pl.BlockSpec((tm, KV_W), tok)] * 2
        out_shape += [jax.ShapeDtypeStruct((batch * seq, KV_W), F32)] * 2
    return pl.pallas_call(
        functools.partial(_in_proj_kernel, use_rope=use_rope, emit_kv=emit_kv),
        grid=(batch, nt),
        in_specs=in_specs,
        out_specs=out_specs,
        out_shape=out_shape,
        compiler_params=_params(),
        name="in_proj",
    )(*args)


def _attn_kernel(*refs, n_src):
    q_ref = refs[0]
    srcs = [(refs[1 + 2 * i], refs[2 + 2 * i]) for i in range(n_src)]
    o_ref = refs[1 + 2 * n_src]
    tq = q_ref.shape[3]

    q = jnp.concatenate([q_ref[0, j] for j in range(HEADS_PER_KV)], axis=1)
    m = acc = None
    for k_ref, v_ref in srcs:
        n_blocks, _, tk = v_ref.shape[2:]
        for j in range(n_blocks):
            kb = k_ref[0, 0, j * tk:(j + 1) * tk, :]
            vb = v_ref[0, 0, j]
            s = jnp.dot(kb, q, preferred_element_type=F32)
            block_max = jnp.max(s, axis=0, keepdims=True)
            m_new = block_max if m is None else jnp.maximum(m, block_max)
            p = jnp.exp2(s - m_new).astype(BF16)
            pv = jnp.dot(vb, p, preferred_element_type=F32)
            acc = pv if m is None else jnp.exp2(m - m_new) * acc + pv
            m = m_new

    out = acc[0:HEAD_DIM] / acc[HEAD_DIM:HEAD_DIM + 1]
    for j in range(HEADS_PER_KV):
        o_ref[0, j] = out[:, j * tq:(j + 1) * tq].astype(BF16)


def _attn_call(q, sources):
    batch, _, _, seq = q.shape
    tq = min(Q_TILE, seq)
    in_specs = [pl.BlockSpec((1, HEADS_PER_KV, HEAD_DIM, tq), lambda b, g, i: (b, g, 0, i))]
    args = [q]
    for k_all, v_all in sources:
        nk = k_all.shape[2]
        nb, _, tk = v_all.shape[2:]
        in_specs.append(pl.BlockSpec((1, 1, nk, HEAD_DIM), lambda b, g, i: (b, g, 0, 0)))
        in_specs.append(pl.BlockSpec((1, 1, nb, V_ROWS, tk), lambda b, g, i: (b, g, 0, 0, 0)))
        args += [k_all, v_all]
    return pl.pallas_call(
        functools.partial(_attn_kernel, n_src=len(sources)),
        grid=(batch, N_KV_HEADS, seq // tq),
        in_specs=in_specs,
        out_specs=pl.BlockSpec((1, HEADS_PER_KV, HEAD_DIM, tq), lambda b, g, i: (b, g, 0, i)),
        out_shape=jax.ShapeDtypeStruct((batch, N_HEADS, HEAD_DIM, seq), BF16),
        compiler_params=_params(),
        name="attn",
    )(*args)


def _router_rows(logits_ref, bias_ref, c_ref):
    lg = logits_ref[...]
    ex = jnp.exp(lg - jnp.max(lg, axis=0, keepdims=True))
    logits_ref[...] = ex / jnp.sum(ex, axis=0, keepdims=True)
    score = [logits_ref[e:e + 1, :] for e in range(N_EXPERTS)]
    sel = [score[e] + bias_ref[e:e + 1, :] for e in range(N_EXPERTS)]

    group_score = []
    for g in range(N_GROUPS):
        four = sel[g * EXPERTS_PER_GROUP:(g + 1) * EXPERTS_PER_GROUP]
        best = None
        for i in range(EXPERTS_PER_GROUP):
            for j in range(i + 1, EXPERTS_PER_GROUP):
                pair = four[i] + four[j]
                best = pair if best is None else jnp.maximum(best, pair)
        group_score.append(best)
    gmax = functools.reduce(jnp.maximum, group_score)

    in_group = []
    free = jnp.ones_like(gmax)
    for g in range(N_GROUPS):
        hit = jnp.where(group_score[g] == gmax, free, 0.0)
        in_group.append(hit)
        free = free - hit

    def pick(values, i):
        out = jnp.zeros_like(gmax)
        for g in range(N_GROUPS):
            out = jnp.where(in_group[g] > 0.0, values[g * EXPERTS_PER_GROUP + i], out)
        return out

    cand_sel = [pick(sel, i) for i in range(EXPERTS_PER_GROUP)]
    cand_score = [pick(score, i) for i in range(EXPERTS_PER_GROUP)]
    chosen = []
    for i in range(EXPERTS_PER_GROUP):
        rank = jnp.zeros_like(gmax)
        for j in range(EXPERTS_PER_GROUP):
            if j < i:
                rank = rank + jnp.where(cand_sel[j] >= cand_sel[i], 1.0, 0.0)
            elif j > i:
                rank = rank + jnp.where(cand_sel[j] > cand_sel[i], 1.0, 0.0)
        chosen.append(jnp.where(rank < 2.0, cand_score[i], 0.0))
    inv = 1.0 / functools.reduce(lambda a, b: a + b, chosen)
    weight = [c * inv for c in chosen]
    for g in range(N_GROUPS):
        for i in range(EXPERTS_PER_GROUP):
            e = g * EXPERTS_PER_GROUP + i
            c_ref[e:e + 1, :] = in_group[g] * weight[i]


def _post_kernel(*refs, has_halo):
    it = iter(refs)
    h_ref, o_ref, a_ref = next(it), next(it), next(it)
    ap_ref = an_ref = None
    if has_halo:
        ap_ref, an_ref = next(it), next(it)
    (gate_ref, mod_ref, wao_ref, wco_ref, wout_ref, cw_ref, cb_ref, lg_ref, lb_ref) = (
        next(it) for _ in range(9))
    hout_ref = next(it)
    xs_ref = next(it)

    tm = h_ref.shape[0]
    g1 = mod_ref[0][:, 2 * D_MODEL:3 * D_MODEL]

    o_t = o_ref[0].reshape(Q_W, tm)
    attn_br = lax.dot_general(o_t, wao_ref[...], TN_DIMS, preferred_element_type=F32)

    zeros_halo = jnp.zeros((CONV_HALO, D_CONV), F32)
    if has_halo:
        t = pl.program_id(1)
        last = pl.num_programs(1) - 1
        before = jnp.where(t > 0, ap_ref[0].astype(F32), zeros_halo)
        after = jnp.where(t < last, an_ref[0].astype(F32), zeros_halo)
    else:
        before = after = zeros_halo
    a_cur = a_ref[0].astype(F32)
    slabs = []
    for s in range(D_CONV // LANES):
        cols = slice(s * LANES, (s + 1) * LANES)
        xs_ref[s, 0:CONV_HALO, :] = before[:, cols]
        xs_ref[s, CONV_HALO:CONV_HALO + tm, :] = a_cur[:, cols]
        xs_ref[s, CONV_HALO + tm:, :] = after[:, cols]
        acc = jnp.zeros((tm, LANES), F32) + cb_ref[:, cols]
        for j in range(CONV_K):
            start = CONV_HALO - CONV_PAD + j
            acc = acc + xs_ref[s, start:start + tm, :] * cw_ref[j:j + 1, cols]
        slabs.append(acc)
    cv = jnp.concatenate(slabs, axis=1)
    mu = jnp.mean(cv, axis=-1, keepdims=True)
    dev = cv - mu
    var = jnp.mean(dev * dev, axis=-1, keepdims=True)
    y = dev * lax.rsqrt(var + EPS) * lg_ref[...] + lb_ref[...]
    y = y * _sigmoid(y)
    conv_br = jnp.dot(y.astype(BF16), wco_ref[...], preferred_element_type=F32)

    gates = gate_ref[...].astype(F32)
    merged = gates[:, 0:D_MODEL] * attn_br + gates[:, D_MODEL:] * conv_br
    out = jnp.dot(merged.astype(BF16), wout_ref[...], preferred_element_type=F32)
    hout_ref[...] = h_ref[...] + g1 * out


def _post_call(h, o_t, a, gates, mod, lw, batch, seq):
    tm = TOKEN_TILE
    nt = seq // tm
    has_halo = nt > 1
    halo_blocks_per_tile = tm // CONV_HALO
    n_halo_blocks = seq // CONV_HALO
    tok = lambda b, t: (b * nt + t, 0)
    const2 = lambda b, t: (0, 0)
    in_specs = [
        pl.BlockSpec((tm, D_MODEL), tok),
        pl.BlockSpec((1, N_HEADS, HEAD_DIM, tm), lambda b, t: (b, 0, 0, t)),
        pl.BlockSpec((1, tm, D_CONV), lambda b, t: (b, t, 0)),
    ]
    args = [h, o_t, a]
    if has_halo:
        in_specs += [
            pl.BlockSpec((1, CONV_HALO, D_CONV),
                         lambda b, t: (b, jnp.maximum(t * halo_blocks_per_tile - 1, 0), 0)),
            pl.BlockSpec((1, CONV_HALO, D_CONV),
                         lambda b, t: (b, jnp.minimum((t + 1) * halo_blocks_per_tile,
                                                      n_halo_blocks - 1), 0)),
        ]
        args += [a, a]
    in_specs += [
        pl.BlockSpec((tm, 2 * D_MODEL), tok),
        pl.BlockSpec((1, 1, N_MOD), lambda b, t: (b if mod.shape[0] > 1 else 0, 0, 0)),
        _resident((Q_W, D_MODEL)),
        _resident((D_CONV, D_MODEL)),
        _resident((D_MODEL, D_MODEL)),
        pl.BlockSpec((CONV_K, D_CONV), const2),
        pl.BlockSpec((1, D_CONV), const2),
        pl.BlockSpec((1, D_CONV), const2),
        pl.BlockSpec((1, D_CONV), const2),
    ]
    args += [gates, mod, lw["w_attn_o"], lw["w_conv_o"], lw["w_out"], lw["conv_w"], lw["conv_b"],
             lw["conv_ln_g"], lw["conv_ln_b"]]
    return pl.pallas_call(
        functools.partial(_post_kernel, has_halo=has_halo),
        grid=(batch, nt),
        in_specs=in_specs,
        out_specs=pl.BlockSpec((tm, D_MODEL), tok),
        out_shape=jax.ShapeDtypeStruct((batch * seq, D_MODEL), F32),
        scratch_shapes=[pltpu.VMEM((D_CONV // LANES, tm + 2 * CONV_HALO, LANES), F32)],
        compiler_params=_params(),
        name="post",
    )(*args)


MOE_ROW_CHUNK = 1024


def _moe_kernel(h_ref, mod_ref, g2_ref, wrh_ref, wrl_ref, rb_ref, wg_ref, wu_ref, wd_ref, fg_ref,
                o_ref, mid_ref, lgt_ref, c_ref, *, final_norm):
    h1 = h_ref[...]
    tm = h1.shape[0]
    mod = mod_ref[0]
    sh2 = mod[:, 3 * D_MODEL:4 * D_MODEL]
    sc2 = mod[:, 4 * D_MODEL:5 * D_MODEL]
    ms = jnp.mean(h1 * h1, axis=-1, keepdims=True)
    u2 = (h1 * lax.rsqrt(ms + EPS) * g2_ref[...]) * (1.0 + sc2) + sh2
    u = u2.astype(BF16)
    u_lo = (u2 - u.astype(F32)).astype(BF16)
    lgt_ref[...] = (
        lax.dot_general(wrh_ref[...], u, NT_DIMS, preferred_element_type=F32)
        + lax.dot_general(wrh_ref[...], u_lo, NT_DIMS, preferred_element_type=F32)
        + lax.dot_general(wrl_ref[...], u, NT_DIMS, preferred_element_type=F32))
    _router_rows(lgt_ref, rb_ref, c_ref)

    experts_per_chunk = MOE_ROW_CHUNK // D_EXPERT
    for r in range(MOE_W // MOE_ROW_CHUNK):
        rows = slice(r * MOE_ROW_CHUNK, (r + 1) * MOE_ROW_CHUNK)
        hg = lax.dot_general(wg_ref[rows, :], u, NT_DIMS, preferred_element_type=F32)
        hu = lax.dot_general(wu_ref[rows, :], u, NT_DIMS, preferred_element_type=F32)
        act = (hg * _sigmoid(hg)) * hu
        act = act.reshape(experts_per_chunk, D_EXPERT, tm)
        c = c_ref[r * experts_per_chunk:(r + 1) * experts_per_chunk, :]
        mid_ref[rows, :] = (act * c[:, None, :]).reshape(MOE_ROW_CHUNK, tm).astype(BF16)
    y_t = jnp.dot(wd_ref[...], mid_ref[...], preferred_element_type=F32)
    g2 = mod[:, 5 * D_MODEL:6 * D_MODEL]
    h2 = h1 + g2 * y_t.T
    if final_norm:
        ms2 = jnp.mean(h2 * h2, axis=-1, keepdims=True)
        h2 = h2 * lax.rsqrt(ms2 + EPS) * fg_ref[...]
    o_ref[...] = h2


def _moe_call(h, mod, lw, shared, batch, seq, final_norm):
    tm = TOKEN_TILE
    nt = seq // tm
    tok = lambda b, t: (b * nt + t, 0)
    const2 = lambda b, t: (0, 0)
    return pl.pallas_call(
        functools.partial(_moe_kernel, final_norm=final_norm),
        grid=(batch, nt),
        in_specs=[
            pl.BlockSpec((tm, D_MODEL), tok),
            pl.BlockSpec((1, 1, N_MOD), lambda b, t: (b if mod.shape[0] > 1 else 0, 0, 0)),
            pl.BlockSpec((1, D_MODEL), const2),
            pl.BlockSpec((N_EXPERTS, D_MODEL), const2),
            pl.BlockSpec((N_EXPERTS, D_MODEL), const2),
            pl.BlockSpec((N_EXPERTS, tm), const2),
            _resident((MOE_W, D_MODEL)),
            _resident((MOE_W, D_MODEL)),
            _resident((D_MODEL, MOE_W)),
            pl.BlockSpec((1, D_MODEL), const2),
        ],
        out_specs=pl.BlockSpec((tm, D_MODEL), tok),
        out_shape=jax.ShapeDtypeStruct((batch * seq, D_MODEL), F32),
        scratch_shapes=[
            pltpu.VMEM((MOE_W, tm), BF16),
            pltpu.VMEM((N_EXPERTS, tm), F32),
            pltpu.VMEM((N_EXPERTS, tm), F32),
        ],
        compiler_params=_params(),
        name="moe",
    )(h, mod, lw["norm2_g"], shared["w_router_hi"], shared["w_router_lo"], shared["router_bias"],
      lw["w_gate_t"], lw["w_up_t"], lw["w_down_t"], shared["final_gain"])


def _rope_tables_t(n_tokens):
    axis_dim = HEAD_DIM // 2
    rows = n_tokens // GRID_W
    row = jnp.repeat(jnp.arange(rows), GRID_W).astype(F32)
    col = jnp.tile(jnp.arange(GRID_W), rows).astype(F32)
    inv = 1.0 / (ROPE_THETA ** (jnp.arange(0, axis_dim, 2, dtype=F32) / axis_dim))
    ar = inv[:, None] * row[None, :]
    ac = inv[:, None] * col[None, :]
    cos = jnp.concatenate([jnp.cos(ar), jnp.cos(ar), jnp.cos(ac), jnp.cos(ac)], axis=0)
    sin = jnp.concatenate([jnp.sin(ar), jnp.sin(ar), jnp.sin(ac), jnp.sin(ac)], axis=0)
    return cos, sin


def _layer_weights(l, w):
    tm = TOKEN_TILE
    scale = math.log2(math.e) / math.sqrt(HEAD_DIM)
    w_in = w["w_in"][l]
    return {
        "norm1_g": w["norm1_g"][l].reshape(1, D_MODEL),
        "w_qkv_t": w_in[:, :QKV_W].T.astype(BF16),
        "w_rest": w_in[:, QKV_W:].astype(BF16),
        "q_gain": jnp.broadcast_to((w["q_norm_g"][l] * scale)[:, None], (HEAD_DIM, tm)),
        "k_gain": jnp.broadcast_to(w["k_norm_g"][l][:, None], (HEAD_DIM, tm)),
        "w_attn_o": w["w_attn_o"][l].astype(BF16),
        "w_conv_o": w["w_conv_o"][l].astype(BF16),
        "w_out": w["w_out"][l].astype(BF16),
        "conv_w": w["conv_w"][l],
        "conv_b": w["conv_b"][l].reshape(1, D_CONV),
        "conv_ln_g": w["conv_ln_g"][l].reshape(1, D_CONV),
        "conv_ln_b": w["conv_ln_b"][l].reshape(1, D_CONV),
        "norm2_g": w["norm2_g"][l].reshape(1, D_MODEL),
        "w_gate_t": w["w_gate"][l].transpose(0, 2, 1).reshape(MOE_W, D_MODEL).astype(BF16),
        "w_up_t": w["w_up"][l].transpose(0, 2, 1).reshape(MOE_W, D_MODEL).astype(BF16),
        "w_down_t": w["w_down"][l].transpose(2, 0, 1).reshape(D_MODEL, MOE_W).astype(BF16),
    }


def _trunk_layer(h, mod, lw, shared, rope, cache, batch, seq, emit_kv, final_norm):
    key_tile = min(KEY_TILE_MAX, seq)
    outs = _in_proj_call(h, mod, lw, rope, batch, seq, key_tile, emit_kv)
    q, k_att, v_att, a, gates = outs[:5]
    sources = ([cache] if cache is not None else []) + [(k_att, v_att)]
    o_t = _attn_call(q, sources)
    h1 = _post_call(h, o_t, a, gates, mod, lw, batch, seq)
    h2 = _moe_call(h1, mod, lw, shared, batch, seq, final_norm)
    return h2, outs[5:]


def kernel(x_prompt, x_sample, cache_k, cache_v, c, c_ctx, norm1_g, w_ada, b_ada, w_in, q_norm_g,
           k_norm_g, w_attn_o, conv_w, conv_b, conv_ln_g, conv_ln_b, w_conv_o, w_out, norm2_g,
           w_router, router_bias, w_gate, w_up, w_down, final_norm_g):
    n_ctx, ctx_len, _ = x_prompt.shape
    n_dec, dec_len, _ = x_sample.shape
    depth = w_in.shape[0]
    weights = dict(norm1_g=norm1_g, w_in=w_in, q_norm_g=q_norm_g, k_norm_g=k_norm_g,
                   w_attn_o=w_attn_o, conv_w=conv_w, conv_b=conv_b, conv_ln_g=conv_ln_g,
                   conv_ln_b=conv_ln_b, w_conv_o=w_conv_o, w_out=w_out, norm2_g=norm2_g,
                   w_gate=w_gate, w_up=w_up, w_down=w_down)

    cvecs = jnp.zeros((ADA_ROWS, D_MODEL), F32).at[0].set(c_ctx).at[1:1 + n_dec].set(c)
    mod_all = _ada_call(cvecs, w_ada, b_ada)
    cache_kp, cache_vp = _cache_call(cache_k, cache_v, min(KEY_TILE_MAX, dec_len))
    rope = _rope_tables_t(dec_len)

    w_router_t = w_router.T
    w_router_hi = w_router_t.astype(BF16)
    shared = {
        "w_router_hi": w_router_hi,
        "w_router_lo": (w_router_t - w_router_hi.astype(F32)).astype(BF16),
        "router_bias": jnp.broadcast_to(router_bias[:, None], (N_EXPERTS, TOKEN_TILE)),
        "final_gain": final_norm_g.reshape(1, D_MODEL),
    }

    hp = x_prompt.reshape(n_ctx * ctx_len, D_MODEL)
    hs = x_sample.reshape(n_dec * dec_len, D_MODEL)
    new_k, new_v = [], []
    for l in range(depth):
        lw = _layer_weights(l, weights)
        last = l == depth - 1
        mod_p = mod_all[l, 0:1].reshape(1, 1, N_MOD)
        mod_s = mod_all[l, 1:1 + n_dec].reshape(n_dec, 1, N_MOD)
        hp, (nk, nv) = _trunk_layer(hp, mod_p, lw, shared, None, None, n_ctx, ctx_len, True, last)
        new_k.append(nk.reshape(n_ctx, ctx_len, N_KV_HEADS, HEAD_DIM))
        new_v.append(nv.reshape(n_ctx, ctx_len, N_KV_HEADS, HEAD_DIM))
        cache = (cache_kp[:, l], cache_vp[:, l])
        hs, _ = _trunk_layer(hs, mod_s, lw, shared, rope, cache, n_dec, dec_len, False, last)

    y_prompt = hp.reshape(n_ctx, ctx_len, D_MODEL)
    y_sample = hs.reshape(n_dec, dec_len, D_MODEL)
    return (y_prompt, y_sample, jnp.stack(new_k, axis=1), jnp.stack(new_v, axis=1))
```

```python
import functools
import math

import jax
import jax.numpy as jnp
import numpy as np
from jax import lax
from jax.experimental import pallas as pl
from jax.experimental.pallas import tpu as pltpu

F32 = jnp.float32
BF16 = jnp.bfloat16

D_MODEL = 1024
N_HEADS = 16
N_KV_HEADS = 4
HEADS_PER_KV = N_HEADS // N_KV_HEADS
LANES = 128
HEAD_DIM = 64
BF16_SUBLANE_TILE = 16
V_ROWS = HEAD_DIM + BF16_SUBLANE_TILE
Q_W = N_HEADS * HEAD_DIM
KV_W = N_KV_HEADS * HEAD_DIM
QKV_W = Q_W + 2 * KV_W
D_CONV = 512
CONV_K = 31
CONV_PAD = CONV_K // 2
CONV_HALO = 16
N_EXPERTS = 16
N_GROUPS = 4
EXPERTS_PER_GROUP = N_EXPERTS // N_GROUPS
D_EXPERT = 256
MOE_W = N_EXPERTS * D_EXPERT
GRID_W = 64
ROPE_THETA = 10000.0
EPS = 1e-6
N_MOD = 6 * D_MODEL
ADA_ROWS = 8

TOKEN_TILE = 512
IN_PROJ_TILE = 256
Q_TILE = 1024
KEY_TILE_MAX = 256
ADA_COL_TILE = 1536
V7X_VMEM_LIMIT_BYTES = 56 * 1024 * 1024

NT_DIMS = (((1,), (1,)), ((), ()))
TN_DIMS = (((0,), (0,)), ((), ()))


def _params():
    return pltpu.CompilerParams(vmem_limit_bytes=V7X_VMEM_LIMIT_BYTES)


def _resident(shape):
    zeros = (0,) * len(shape)
    return pl.BlockSpec(shape, lambda *_: zeros, pipeline_mode=pl.Buffered(1))


def _sigmoid(x):
    return jax.nn.sigmoid(x)


def _ada_kernel(c_ref, w_ref, b_ref, o_ref):
    c = c_ref[...]
    x = (c * _sigmoid(c)).astype(BF16)
    w = w_ref[0].astype(BF16)
    o_ref[0] = jnp.dot(x, w, preferred_element_type=F32) + b_ref[0]


def _ada_call(cvecs, w_ada, b_ada):
    depth = w_ada.shape[0]
    return pl.pallas_call(
        _ada_kernel,
        grid=(depth, N_MOD // ADA_COL_TILE),
        in_specs=[
            pl.BlockSpec((ADA_ROWS, D_MODEL), lambda l, j: (0, 0)),
            pl.BlockSpec((1, D_MODEL, ADA_COL_TILE), lambda l, j: (l, 0, j)),
            pl.BlockSpec((1, 1, ADA_COL_TILE), lambda l, j: (l, 0, j)),
        ],
        out_specs=pl.BlockSpec((1, ADA_ROWS, ADA_COL_TILE), lambda l, j: (l, 0, j)),
        out_shape=jax.ShapeDtypeStruct((depth, ADA_ROWS, N_MOD), F32),
        compiler_params=_params(),
        name="ada_mod",
    )(cvecs, w_ada, b_ada.reshape(depth, 1, N_MOD))


def _cache_kernel(k_ref, v_ref, ko_ref, vo_ref):
    k = k_ref[0, 0]
    vt = v_ref[0, 0].T
    n_blocks, _, tk = vo_ref.shape[3:]
    ones = jnp.ones((V_ROWS - HEAD_DIM, tk), BF16)
    for h in range(N_KV_HEADS):
        lo = h * HEAD_DIM
        ko_ref[0, 0, h] = k[:, lo:lo + HEAD_DIM].astype(BF16)
        for j in range(n_blocks):
            vo_ref[0, 0, h, j, 0:HEAD_DIM, :] = (
                vt[lo:lo + HEAD_DIM, j * tk:(j + 1) * tk].astype(BF16))
            vo_ref[0, 0, h, j, HEAD_DIM:V_ROWS, :] = ones


def _cache_call(cache_k, cache_v, key_tile):
    b, depth, past = cache_k.shape[:3]
    n_blocks = past // key_tile
    ck = cache_k.reshape(b, depth, past, KV_W)
    cv = cache_v.reshape(b, depth, past, KV_W)
    spec_in = pl.BlockSpec((1, 1, past, KV_W), lambda i, l: (i, l, 0, 0))
    return pl.pallas_call(
        _cache_kernel,
        grid=(b, depth),
        in_specs=[spec_in, spec_in],
        out_specs=[
            pl.BlockSpec((1, 1, N_KV_HEADS, past, HEAD_DIM), lambda i, l: (i, l, 0, 0, 0)),
            pl.BlockSpec((1, 1, N_KV_HEADS, n_blocks, V_ROWS, key_tile),
                         lambda i, l: (i, l, 0, 0, 0, 0)),
        ],
        out_shape=[
            jax.ShapeDtypeStruct((b, depth, N_KV_HEADS, past, HEAD_DIM), BF16),
            jax.ShapeDtypeStruct((b, depth, N_KV_HEADS, n_blocks, V_ROWS, key_tile), BF16),
        ],
        compiler_params=_params(),
        name="cache_layout",
    )(ck, cv)


def _rope_rows(x, cos, sin):
    q4 = HEAD_DIM // 4
    a1, a2, b1, b2 = (x[:, i * q4:(i + 1) * q4] for i in range(4))
    rot = jnp.concatenate([-a2, a1, -b2, b1], axis=1)
    return x * cos[None] + rot * sin[None]


def _head_rmsnorm_rows(x, gain):
    ms = jnp.mean(x * x, axis=1, keepdims=True)
    return x * lax.rsqrt(ms + EPS) * gain[None]


def _in_proj_kernel(*refs, use_rope, emit_kv):
    it = iter(refs)
    h_ref, mod_ref, g1_ref, wqkv_ref, wrest_ref, qg_ref, kg_ref = (next(it) for _ in range(7))
    cos_ref = sin_ref = None
    if use_rope:
        cos_ref, sin_ref = next(it), next(it)
    q_ref, k_ref, v_ref, a_ref, gate_ref = (next(it) for _ in range(5))
    nk_ref = nv_ref = None
    if emit_kv:
        nk_ref, nv_ref = next(it), next(it)

    tm = h_ref.shape[0]
    x = h_ref[...]
    mod = mod_ref[0]
    sh1 = mod[:, 0:D_MODEL]
    sc1 = mod[:, D_MODEL:2 * D_MODEL]
    ms = jnp.mean(x * x, axis=-1, keepdims=True)
    u = (x * lax.rsqrt(ms + EPS) * g1_ref[...]) * (1.0 + sc1) + sh1
    ub = u.astype(BF16)

    q_t = lax.dot_general(wqkv_ref[0:Q_W, :], ub, NT_DIMS, preferred_element_type=F32)
    q3 = _head_rmsnorm_rows(q_t.reshape(N_HEADS, HEAD_DIM, tm), qg_ref[...])
    if use_rope:
        q3 = _rope_rows(q3, cos_ref[...], sin_ref[...])
    q_ref[0] = q3.astype(BF16)

    k_t = lax.dot_general(wqkv_ref[Q_W:Q_W + KV_W, :], ub, NT_DIMS, preferred_element_type=F32)
    k3 = _head_rmsnorm_rows(k_t.reshape(N_KV_HEADS, HEAD_DIM, tm), kg_ref[...])
    if emit_kv:
        nk_ref[...] = k3.reshape(KV_W, tm).T
    if use_rope:
        k3 = _rope_rows(k3, cos_ref[...], sin_ref[...])
    k_tok = k3.reshape(KV_W, tm).T.astype(BF16)
    for hh in range(N_KV_HEADS):
        k_ref[0, hh] = k_tok[:, hh * HEAD_DIM:(hh + 1) * HEAD_DIM]

    v_t = lax.dot_general(wqkv_ref[Q_W + KV_W:QKV_W, :], ub, NT_DIMS, preferred_element_type=F32)
    v3 = v_t.reshape(N_KV_HEADS, HEAD_DIM, tm).astype(BF16)
    n_vb, _, v_lanes = v_ref.shape[2:]
    for j in range(n_vb):
        v_ref[0, :, j, 0:HEAD_DIM, :] = v3[:, :, j * v_lanes:(j + 1) * v_lanes]
        v_ref[0, :, j, HEAD_DIM:V_ROWS, :] = jnp.ones(
            (N_KV_HEADS, V_ROWS - HEAD_DIM, v_lanes), BF16)
    if emit_kv:
        nv_ref[...] = v_t.T

    glu = jnp.dot(ub, wrest_ref[:, 0:2 * D_CONV], preferred_element_type=F32)
    a_ref[0] = (glu[:, 0:D_CONV] * _sigmoid(glu[:, D_CONV:2 * D_CONV])).astype(BF16)
    gates = jnp.dot(ub, wrest_ref[:, 2 * D_CONV:], preferred_element_type=F32)
    gate_ref[...] = _sigmoid(gates).astype(BF16)


def _in_proj_call(h, mod, lw, rope, batch, seq, key_tile, emit_kv):
    tm = min(IN_PROJ_TILE, seq)
    nt = seq // tm
    n_key_blocks = seq // key_tile
    if tm >= key_tile:
        v_block = (1, N_KV_HEADS, tm // key_tile, V_ROWS, key_tile)
        v_index = lambda b, t: (b, 0, t, 0, 0)
    else:
        tiles_per_key = key_tile // tm
        v_block = (1, N_KV_HEADS, 1, V_ROWS, tm)
        v_index = lambda b, t: (b, 0, t // tiles_per_key, 0, t % tiles_per_key)
    use_rope = rope is not None
    tok = lambda b, t: (b * nt + t, 0)
    const2 = lambda b, t: (0, 0)
    in_specs = [
        pl.BlockSpec((tm, D_MODEL), tok),
        pl.BlockSpec((1, 1, N_MOD), lambda b, t: (b if mod.shape[0] > 1 else 0, 0, 0)),
        pl.BlockSpec((1, D_MODEL), const2),
        _resident((QKV_W, D_MODEL)),
        _resident((D_MODEL, 2 * D_CONV + 2 * D_MODEL)),
        pl.BlockSpec((HEAD_DIM, tm), const2),
        pl.BlockSpec((HEAD_DIM, tm), const2),
    ]
    args = [h, mod, lw["norm1_g"], lw["w_qkv_t"], lw["w_rest"],
            lw["q_gain"][:, :tm], lw["k_gain"][:, :tm]]
    if use_rope:
        in_specs += [pl.BlockSpec((HEAD_DIM, tm), lambda b, t: (0, t))] * 2
        args += list(rope)
    out_specs = [
        pl.BlockSpec((1, N_HEADS, HEAD_DIM, tm), lambda b, t: (b, 0, 0, t)),
        pl.BlockSpec((1, N_KV_HEADS, tm, HEAD_DIM), lambda b, t: (b, 0, t, 0)),
        pl.BlockSpec(v_block, v_index),
        pl.BlockSpec((1, tm, D_CONV), lambda b, t: (b, t, 0)),
        pl.BlockSpec((tm, 2 * D_MODEL), tok),
    ]
    out_shape = [
        jax.ShapeDtypeStruct((batch, N_HEADS, HEAD_DIM, seq), BF16),
        jax.ShapeDtypeStruct((batch, N_KV_HEADS, seq, HEAD_DIM), BF16),
        jax.ShapeDtypeStruct((batch, N_KV_HEADS, n_key_blocks, V_ROWS, key_tile), BF16),
        jax.ShapeDtypeStruct((batch, seq, D_CONV), BF16),
        jax.ShapeDtypeStruct((batch * seq, 2 * D_MODEL), BF16),
    ]
    if emit_kv:
        out_specs += [pl.BlockSpec((tm, KV_W), tok)] * 2
        out_shape += [jax.ShapeDtypeStruct((batch * seq, KV_W), F32)] * 2
    return pl.pallas_call(
        functools.partial(_in_proj_kernel, use_rope=use_rope, emit_kv=emit_kv),
        grid=(batch, nt),
        in_specs=in_specs,
        out_specs=out_specs,
        out_shape=out_shape,
        compiler_params=_params(),
        name="in_proj",
    )(*args)


def _attn_kernel(*refs, n_src):
    q_ref = refs[0]
    srcs = [(refs[1 + 2 * i], refs[2 + 2 * i]) for i in range(n_src)]
    o_ref = refs[1 + 2 * n_src]
    tq = q_ref.shape[3]

    q = jnp.concatenate([q_ref[0, j] for j in range(HEADS_PER_KV)], axis=1)
    m = acc = None
    for k_ref, v_ref in srcs:
        n_blocks, _, tk = v_ref.shape[2:]
        for j in range(n_blocks):
            kb = k_ref[0, 0, j * tk:(j + 1) * tk, :]
            vb = v_ref[0, 0, j]
            s = jnp.dot(kb, q, preferred_element_type=F32)
            block_max = jnp.max(s, axis=0, keepdims=True)
            m_new = block_max if m is None else jnp.maximum(m, block_max)
            p = jnp.exp2(s - m_new).astype(BF16)
            pv = jnp.dot(vb, p, preferred_element_type=F32)
            acc = pv if m is None else jnp.exp2(m - m_new) * acc + pv
            m = m_new

    out = acc[0:HEAD_DIM] / acc[HEAD_DIM:HEAD_DIM + 1]
    for j in range(HEADS_PER_KV):
        o_ref[0, j] = out[:, j * tq:(j + 1) * tq].astype(BF16)


def _attn_call(q, sources):
    batch, _, _, seq = q.shape
    tq = min(Q_TILE, seq)
    in_specs = [pl.BlockSpec((1, HEADS_PER_KV, HEAD_DIM, tq), lambda b, g, i: (b, g, 0, i))]
    args = [q]
    for k_all, v_all in sources:
        nk = k_all.shape[2]
        nb, _, tk = v_all.shape[2:]
        in_specs.append(pl.BlockSpec((1, 1, nk, HEAD_DIM), lambda b, g, i: (b, g, 0, 0)))
        in_specs.append(pl.BlockSpec((1, 1, nb, V_ROWS, tk), lambda b, g, i: (b, g, 0, 0, 0)))
        args += [k_all, v_all]
    return pl.pallas_call(
        functools.partial(_attn_kernel, n_src=len(sources)),
        grid=(batch, N_KV_HEADS, seq // tq),
        in_specs=in_specs,
        out_specs=pl.BlockSpec((1, HEADS_PER_KV, HEAD_DIM, tq), lambda b, g, i: (b, g, 0, i)),
        out_shape=jax.ShapeDtypeStruct((batch, N_HEADS, HEAD_DIM, seq), BF16),
        compiler_params=_params(),
        name="attn",
    )(*args)


def _router_rows(logits_ref, bias_ref, c_ref):
    lg = logits_ref[...]
    ex = jnp.exp(lg - jnp.max(lg, axis=0, keepdims=True))
    logits_ref[...] = ex / jnp.sum(ex, axis=0, keepdims=True)
    score = [logits_ref[e:e + 1, :] for e in range(N_EXPERTS)]
    sel = [score[e] + bias_ref[e:e + 1, :] for e in range(N_EXPERTS)]

    group_score = []
    for g in range(N_GROUPS):
        four = sel[g * EXPERTS_PER_GROUP:(g + 1) * EXPERTS_PER_GROUP]
        best = None
        for i in range(EXPERTS_PER_GROUP):
            for j in range(i + 1, EXPERTS_PER_GROUP):
                pair = four[i] + four[j]
                best = pair if best is None else jnp.maximum(best, pair)
        group_score.append(best)
    gmax = functools.reduce(jnp.maximum, group_score)

    in_group = []
    free = jnp.ones_like(gmax)
    for g in range(N_GROUPS):
        hit = jnp.where(group_score[g] == gmax, free, 0.0)
        in_group.append(hit)
        free = free - hit

    def pick(values, i):
        out = jnp.zeros_like(gmax)
        for g in range(N_GROUPS):
            out = jnp.where(in_group[g] > 0.0, values[g * EXPERTS_PER_GROUP + i], out)
        return out

    cand_sel = [pick(sel, i) for i in range(EXPERTS_PER_GROUP)]
    cand_score = [pick(score, i) for i in range(EXPERTS_PER_GROUP)]
    chosen = []
    for i in range(EXPERTS_PER_GROUP):
        rank = jnp.zeros_like(gmax)
        for j in range(EXPERTS_PER_GROUP):
            if j < i:
                rank = rank + jnp.where(cand_sel[j] >= cand_sel[i], 1.0, 0.0)
            elif j > i:
                rank = rank + jnp.where(cand_sel[j] > cand_sel[i], 1.0, 0.0)
        chosen.append(jnp.where(rank < 2.0, cand_score[i], 0.0))
    inv = 1.0 / functools.reduce(lambda a, b: a + b, chosen)
    weight = [c * inv for c in chosen]
    for g in range(N_GROUPS):
        for i in range(EXPERTS_PER_GROUP):
            e = g * EXPERTS_PER_GROUP + i
            c_ref[e:e + 1, :] = in_group[g] * weight[i]


def _post_kernel(*refs, has_halo):
    it = iter(refs)
    h_ref, o_ref, a_ref = next(it), next(it), next(it)
    ap_ref = an_ref = None
    if has_halo:
        ap_ref, an_ref = next(it), next(it)
    (gate_ref, mod_ref, wao_ref, wco_ref, wout_ref, cw_ref, cb_ref, lg_ref, lb_ref) = (
        next(it) for _ in range(9))
    hout_ref = next(it)
    xs_ref = next(it)

    tm = h_ref.shape[0]
    g1 = mod_ref[0][:, 2 * D_MODEL:3 * D_MODEL]

    o_t = o_ref[0].reshape(Q_W, tm)
    attn_br = lax.dot_general(o_t, wao_ref[...], TN_DIMS, preferred_element_type=F32)

    zeros_halo = jnp.zeros((CONV_HALO, D_CONV), F32)
    if has_halo:
        t = pl.program_id(1)
        last = pl.num_programs(1) - 1
        before = jnp.where(t > 0, ap_ref[0].astype(F32), zeros_halo)
        after = jnp.where(t < last, an_ref[0].astype(F32), zeros_halo)
    else:
        before = after = zeros_halo
    a_cur = a_ref[0].astype(F32)
    slabs = []
    for s in range(D_CONV // LANES):
        cols = slice(s * LANES, (s + 1) * LANES)
        xs_ref[s, 0:CONV_HALO, :] = before[:, cols]
        xs_ref[s, CONV_HALO:CONV_HALO + tm, :] = a_cur[:, cols]
        xs_ref[s, CONV_HALO + tm:, :] = after[:, cols]
        acc = jnp.zeros((tm, LANES), F32) + cb_ref[:, cols]
        for j in range(CONV_K):
            start = CONV_HALO - CONV_PAD + j
            acc = acc + xs_ref[s, start:start + tm, :] * cw_ref[j:j + 1, cols]
        slabs.append(acc)
    cv = jnp.concatenate(slabs, axis=1)
    mu = jnp.mean(cv, axis=-1, keepdims=True)
    dev = cv - mu
    var = jnp.mean(dev * dev, axis=-1, keepdims=True)
    y = dev * lax.rsqrt(var + EPS) * lg_ref[...] + lb_ref[...]
    y = y * _sigmoid(y)
    conv_br = jnp.dot(y.astype(BF16), wco_ref[...], preferred_element_type=F32)

    gates = gate_ref[...].astype(F32)
    merged = gates[:, 0:D_MODEL] * attn_br + gates[:, D_MODEL:] * conv_br
    out = jnp.dot(merged.astype(BF16), wout_ref[...], preferred_element_type=F32)
    hout_ref[...] = h_ref[...] + g1 * out


def _post_call(h, o_t, a, gates, mod, lw, batch, seq):
    tm = min(TOKEN_TILE, seq)
    nt = seq // tm
    has_halo = nt > 1
    halo_blocks_per_tile = tm // CONV_HALO
    n_halo_blocks = seq // CONV_HALO
    tok = lambda b, t: (b * nt + t, 0)
    const2 = lambda b, t: (0, 0)
    in_specs = [
        pl.BlockSpec((tm, D_MODEL), tok),
        pl.BlockSpec((1, N_HEADS, HEAD_DIM, tm), lambda b, t: (b, 0, 0, t)),
        pl.BlockSpec((1, tm, D_CONV), lambda b, t: (b, t, 0)),
    ]
    args = [h, o_t, a]
    if has_halo:
        in_specs += [
            pl.BlockSpec((1, CONV_HALO, D_CONV),
                         lambda b, t: (b, jnp.maximum(t * halo_blocks_per_tile - 1, 0), 0)),
            pl.BlockSpec((1, CONV_HALO, D_CONV),
                         lambda b, t: (b, jnp.minimum((t + 1) * halo_blocks_per_tile,
                                                      n_halo_blocks - 1), 0)),
        ]
        args += [a, a]
    in_specs += [
        pl.BlockSpec((tm, 2 * D_MODEL), tok),
        pl.BlockSpec((1, 1, N_MOD), lambda b, t: (b if mod.shape[0] > 1 else 0, 0, 0)),
        _resident((Q_W, D_MODEL)),
        _resident((D_CONV, D_MODEL)),
        _resident((D_MODEL, D_MODEL)),
        pl.BlockSpec((CONV_K, D_CONV), const2),
        pl.BlockSpec((1, D_CONV), const2),
        pl.BlockSpec((1, D_CONV), const2),
        pl.BlockSpec((1, D_CONV), const2),
    ]
    args += [gates, mod, lw["w_attn_o"], lw["w_conv_o"], lw["w_out"], lw["conv_w"], lw["conv_b"],
             lw["conv_ln_g"], lw["conv_ln_b"]]
    return pl.pallas_call(
        functools.partial(_post_kernel, has_halo=has_halo),
        grid=(batch, nt),
        in_specs=in_specs,
        out_specs=pl.BlockSpec((tm, D_MODEL), tok),
        out_shape=jax.ShapeDtypeStruct((batch * seq, D_MODEL), F32),
        scratch_shapes=[pltpu.VMEM((D_CONV // LANES, tm + 2 * CONV_HALO, LANES), F32)],
        compiler_params=_params(),
        name="post",
    )(*args)


MOE_ROW_CHUNK = 1024


def _moe_kernel(h_ref, mod_ref, g2_ref, wrh_ref, wrl_ref, rb_ref, wg_ref, wu_ref, wd_ref, fg_ref,
                o_ref, mid_ref, lgt_ref, c_ref, *, final_norm):
    h1 = h_ref[...]
    tm = h1.shape[0]
    mod = mod_ref[0]
    sh2 = mod[:, 3 * D_MODEL:4 * D_MODEL]
    sc2 = mod[:, 4 * D_MODEL:5 * D_MODEL]
    ms = jnp.mean(h1 * h1, axis=-1, keepdims=True)
    u2 = (h1 * lax.rsqrt(ms + EPS) * g2_ref[...]) * (1.0 + sc2) + sh2
    u = u2.astype(BF16)
    u_lo = (u2 - u.astype(F32)).astype(BF16)
    lgt_ref[...] = (
        lax.dot_general(wrh_ref[...], u, NT_DIMS, preferred_element_type=F32)
        + lax.dot_general(wrh_ref[...], u_lo, NT_DIMS, preferred_element_type=F32)
        + lax.dot_general(wrl_ref[...], u, NT_DIMS, preferred_element_type=F32))
    _router_rows(lgt_ref, rb_ref, c_ref)

    experts_per_chunk = MOE_ROW_CHUNK // D_EXPERT
    for r in range(MOE_W // MOE_ROW_CHUNK):
        rows = slice(r * MOE_ROW_CHUNK, (r + 1) * MOE_ROW_CHUNK)
        hg = lax.dot_general(wg_ref[rows, :], u, NT_DIMS, preferred_element_type=F32)
        hu = lax.dot_general(wu_ref[rows, :], u, NT_DIMS, preferred_element_type=F32)
        act = (hg * _sigmoid(hg)) * hu
        act = act.reshape(experts_per_chunk, D_EXPERT, tm)
        c = c_ref[r * experts_per_chunk:(r + 1) * experts_per_chunk, :]
        mid_ref[rows, :] = (act * c[:, None, :]).reshape(MOE_ROW_CHUNK, tm).astype(BF16)
    y_t = jnp.dot(wd_ref[...], mid_ref[...], preferred_element_type=F32)
    g2 = mod[:, 5 * D_MODEL:6 * D_MODEL]
    h2 = h1 + g2 * y_t.T
    if final_norm:
        ms2 = jnp.mean(h2 * h2, axis=-1, keepdims=True)
        h2 = h2 * lax.rsqrt(ms2 + EPS) * fg_ref[...]
    o_ref[...] = h2


def _moe_call(h, mod, lw, shared, batch, seq, final_norm):
    tm = min(TOKEN_TILE, seq)
    nt = seq // tm
    tok = lambda b, t: (b * nt + t, 0)
    const2 = lambda b, t: (0, 0)
    return pl.pallas_call(
        functools.partial(_moe_kernel, final_norm=final_norm),
        grid=(batch, nt),
        in_specs=[
            pl.BlockSpec((tm, D_MODEL), tok),
            pl.BlockSpec((1, 1, N_MOD), lambda b, t: (b if mod.shape[0] > 1 else 0, 0, 0)),
            pl.BlockSpec((1, D_MODEL), const2),
            pl.BlockSpec((N_EXPERTS, D_MODEL), const2),
            pl.BlockSpec((N_EXPERTS, D_MODEL), const2),
            pl.BlockSpec((N_EXPERTS, tm), const2),
            _resident((MOE_W, D_MODEL)),
            _resident((MOE_W, D_MODEL)),
            _resident((D_MODEL, MOE_W)),
            pl.BlockSpec((1, D_MODEL), const2),
        ],
        out_specs=pl.BlockSpec((tm, D_MODEL), tok),
        out_shape=jax.ShapeDtypeStruct((batch * seq, D_MODEL), F32),
        scratch_shapes=[
            pltpu.VMEM((MOE_W, tm), BF16),
            pltpu.VMEM((N_EXPERTS, tm), F32),
            pltpu.VMEM((N_EXPERTS, tm), F32),
        ],
        compiler_params=_params(),
        name="moe",
    )(h, mod, lw["norm2_g"], shared["w_router_hi"], shared["w_router_lo"],
      shared["router_bias"][:, :tm], lw["w_gate_t"], lw["w_up_t"], lw["w_down_t"],
      shared["final_gain"])


def _rope_tables_t(n_tokens):
    axis_dim = HEAD_DIM // 2
    rows = n_tokens // GRID_W
    row = np.repeat(np.arange(rows), GRID_W).astype(np.float64)
    col = np.tile(np.arange(GRID_W), rows).astype(np.float64)
    inv = 1.0 / (ROPE_THETA ** (np.arange(0, axis_dim, 2, dtype=np.float64) / axis_dim))
    ar = inv[:, None] * row[None, :]
    ac = inv[:, None] * col[None, :]
    cos = np.concatenate([np.cos(ar), np.cos(ar), np.cos(ac), np.cos(ac)], axis=0)
    sin = np.concatenate([np.sin(ar), np.sin(ar), np.sin(ac), np.sin(ac)], axis=0)
    return jnp.asarray(cos, F32), jnp.asarray(sin, F32)


def _layer_weights(l, w):
    tm = IN_PROJ_TILE
    scale = math.log2(math.e) / math.sqrt(HEAD_DIM)
    w_in = w["w_in"][l]
    return {
        "norm1_g": w["norm1_g"][l].reshape(1, D_MODEL),
        "w_qkv_t": w_in[:, :QKV_W].T.astype(BF16),
        "w_rest": w_in[:, QKV_W:].astype(BF16),
        "q_gain": jnp.broadcast_to((w["q_norm_g"][l] * scale)[:, None], (HEAD_DIM, tm)),
        "k_gain": jnp.broadcast_to(w["k_norm_g"][l][:, None], (HEAD_DIM, tm)),
        "w_attn_o": w["w_attn_o"][l].astype(BF16),
        "w_conv_o": w["w_conv_o"][l].astype(BF16),
        "w_out": w["w_out"][l].astype(BF16),
        "conv_w": w["conv_w"][l],
        "conv_b": w["conv_b"][l].reshape(1, D_CONV),
        "conv_ln_g": w["conv_ln_g"][l].reshape(1, D_CONV),
        "conv_ln_b": w["conv_ln_b"][l].reshape(1, D_CONV),
        "norm2_g": w["norm2_g"][l].reshape(1, D_MODEL),
        "w_gate_t": w["w_gate"][l].transpose(0, 2, 1).reshape(MOE_W, D_MODEL).astype(BF16),
        "w_up_t": w["w_up"][l].transpose(0, 2, 1).reshape(MOE_W, D_MODEL).astype(BF16),
        "w_down_t": w["w_down"][l].transpose(2, 0, 1).reshape(D_MODEL, MOE_W).astype(BF16),
    }


def _trunk_layer(h, mod, lw, shared, rope, cache, batch, seq, emit_kv, final_norm):
    key_tile = min(KEY_TILE_MAX, seq)
    outs = _in_proj_call(h, mod, lw, rope, batch, seq, key_tile, emit_kv)
    q, k_att, v_att, a, gates = outs[:5]
    sources = ([cache] if cache is not None else []) + [(k_att, v_att)]
    o_t = _attn_call(q, sources)
    h1 = _post_call(h, o_t, a, gates, mod, lw, batch, seq)
    h2 = _moe_call(h1, mod, lw, shared, batch, seq, final_norm)
    return h2, outs[5:]


def kernel(x_prompt, x_sample, cache_k, cache_v, c, c_ctx, norm1_g, w_ada, b_ada, w_in, q_norm_g,
           k_norm_g, w_attn_o, conv_w, conv_b, conv_ln_g, conv_ln_b, w_conv_o, w_out, norm2_g,
           w_router, router_bias, w_gate, w_up, w_down, final_norm_g):
    n_ctx, ctx_len, _ = x_prompt.shape
    n_dec, dec_len, _ = x_sample.shape
    depth = w_in.shape[0]
    weights = dict(norm1_g=norm1_g, w_in=w_in, q_norm_g=q_norm_g, k_norm_g=k_norm_g,
                   w_attn_o=w_attn_o, conv_w=conv_w, conv_b=conv_b, conv_ln_g=conv_ln_g,
                   conv_ln_b=conv_ln_b, w_conv_o=w_conv_o, w_out=w_out, norm2_g=norm2_g,
                   w_gate=w_gate, w_up=w_up, w_down=w_down)

    cvecs = jnp.zeros((ADA_ROWS, D_MODEL), F32).at[0].set(c_ctx).at[1:1 + n_dec].set(c)
    mod_all = _ada_call(cvecs, w_ada, b_ada)
    cache_kp, cache_vp = _cache_call(cache_k, cache_v, min(KEY_TILE_MAX, dec_len))
    rope = _rope_tables_t(dec_len)

    w_router_t = w_router.T
    w_router_hi = w_router_t.astype(BF16)
    shared = {
        "w_router_hi": w_router_hi,
        "w_router_lo": (w_router_t - w_router_hi.astype(F32)).astype(BF16),
        "router_bias": jnp.broadcast_to(router_bias[:, None], (N_EXPERTS, TOKEN_TILE)),
        "final_gain": final_norm_g.reshape(1, D_MODEL),
    }

    hp = x_prompt.reshape(n_ctx * ctx_len, D_MODEL)
    hs = x_sample.reshape(n_dec * dec_len, D_MODEL)
    new_k, new_v = [], []
    for l in range(depth):
        lw = _layer_weights(l, weights)
        last = l == depth - 1
        mod_p = mod_all[l, 0:1].reshape(1, 1, N_MOD)
        mod_s = mod_all[l, 1:1 + n_dec].reshape(n_dec, 1, N_MOD)
        hp, (nk, nv) = _trunk_layer(hp, mod_p, lw, shared, None, None, n_ctx, ctx_len, True, last)
        new_k.append(nk.reshape(n_ctx, ctx_len, N_KV_HEADS, HEAD_DIM))
        new_v.append(nv.reshape(n_ctx, ctx_len, N_KV_HEADS, HEAD_DIM))
        cache = (cache_kp[:, l], cache_vp[:, l])
        hs, _ = _trunk_layer(hs, mod_s, lw, shared, rope, cache, n_dec, dec_len, False, last)

    y_prompt = hp.reshape(n_ctx, ctx_len, D_MODEL)
    y_sample = hs.reshape(n_dec, dec_len, D_MODEL)
    return (y_prompt, y_sample, jnp.stack(new_k, axis=1), jnp.stack(new_v, axis=1))
```

```python
import functools
import math

import jax
import jax.numpy as jnp
import numpy as np
from jax import lax
from jax.experimental import pallas as pl
from jax.experimental.pallas import tpu as pltpu

F32 = jnp.float32
BF16 = jnp.bfloat16

D_MODEL = 1024
N_HEADS = 16
N_KV_HEADS = 4
HEADS_PER_KV = N_HEADS // N_KV_HEADS
LANES = 128
HEAD_DIM = 64
BF16_SUBLANE_TILE = 16
V_ROWS = HEAD_DIM + BF16_SUBLANE_TILE
Q_W = N_HEADS * HEAD_DIM
KV_W = N_KV_HEADS * HEAD_DIM
QKV_W = Q_W + 2 * KV_W
D_CONV = 512
CONV_K = 31
CONV_PAD = CONV_K // 2
CONV_HALO = 16
N_EXPERTS = 16
N_GROUPS = 4
EXPERTS_PER_GROUP = N_EXPERTS // N_GROUPS
D_EXPERT = 256
MOE_W = N_EXPERTS * D_EXPERT
GRID_W = 64
ROPE_THETA = 10000.0
EPS = 1e-6
N_MOD = 6 * D_MODEL
ADA_ROWS = 8

TOKEN_TILE = 512
IN_PROJ_TILE = 256
Q_TILE = 2048
KEY_TILE_MAX = 256
ADA_COL_TILE = 1536
V7X_VMEM_LIMIT_BYTES = 56 * 1024 * 1024

NT_DIMS = (((1,), (1,)), ((), ()))
TN_DIMS = (((0,), (0,)), ((), ()))


def _params():
    return pltpu.CompilerParams(vmem_limit_bytes=V7X_VMEM_LIMIT_BYTES)


def _resident(shape):
    zeros = (0,) * len(shape)
    return pl.BlockSpec(shape, lambda *_: zeros, pipeline_mode=pl.Buffered(1))


def _sigmoid(x):
    return jax.nn.sigmoid(x)


def _ada_kernel(c_ref, w_ref, b_ref, o_ref):
    c = c_ref[...]
    x = (c * _sigmoid(c)).astype(BF16)
    w = w_ref[0].astype(BF16)
    o_ref[0] = jnp.dot(x, w, preferred_element_type=F32) + b_ref[0]


def _ada_call(cvecs, w_ada, b_ada):
    depth = w_ada.shape[0]
    return pl.pallas_call(
        _ada_kernel,
        grid=(depth, N_MOD // ADA_COL_TILE),
        in_specs=[
            pl.BlockSpec((ADA_ROWS, D_MODEL), lambda l, j: (0, 0)),
            pl.BlockSpec((1, D_MODEL, ADA_COL_TILE), lambda l, j: (l, 0, j)),
            pl.BlockSpec((1, 1, ADA_COL_TILE), lambda l, j: (l, 0, j)),
        ],
        out_specs=pl.BlockSpec((1, ADA_ROWS, ADA_COL_TILE), lambda l, j: (l, 0, j)),
        out_shape=jax.ShapeDtypeStruct((depth, ADA_ROWS, N_MOD), F32),
        compiler_params=_params(),
        name="ada_mod",
    )(cvecs, w_ada, b_ada.reshape(depth, 1, N_MOD))


def _cache_kernel(k_ref, v_ref, ko_ref, vo_ref):
    k = k_ref[0, 0]
    vt = v_ref[0, 0].T
    n_blocks, _, tk = vo_ref.shape[3:]
    ones = jnp.ones((V_ROWS - HEAD_DIM, tk), BF16)
    for h in range(N_KV_HEADS):
        lo = h * HEAD_DIM
        ko_ref[0, 0, h] = k[:, lo:lo + HEAD_DIM].astype(BF16)
        for j in range(n_blocks):
            vo_ref[0, 0, h, j, 0:HEAD_DIM, :] = (
                vt[lo:lo + HEAD_DIM, j * tk:(j + 1) * tk].astype(BF16))
            vo_ref[0, 0, h, j, HEAD_DIM:V_ROWS, :] = ones


def _cache_call(cache_k, cache_v, key_tile):
    b, depth, past = cache_k.shape[:3]
    n_blocks = past // key_tile
    ck = cache_k.reshape(b, depth, past, KV_W)
    cv = cache_v.reshape(b, depth, past, KV_W)
    spec_in = pl.BlockSpec((1, 1, past, KV_W), lambda i, l: (i, l, 0, 0))
    return pl.pallas_call(
        _cache_kernel,
        grid=(b, depth),
        in_specs=[spec_in, spec_in],
        out_specs=[
            pl.BlockSpec((1, 1, N_KV_HEADS, past, HEAD_DIM), lambda i, l: (i, l, 0, 0, 0)),
            pl.BlockSpec((1, 1, N_KV_HEADS, n_blocks, V_ROWS, key_tile),
                         lambda i, l: (i, l, 0, 0, 0, 0)),
        ],
        out_shape=[
            jax.ShapeDtypeStruct((b, depth, N_KV_HEADS, past, HEAD_DIM), BF16),
            jax.ShapeDtypeStruct((b, depth, N_KV_HEADS, n_blocks, V_ROWS, key_tile), BF16),
        ],
        compiler_params=_params(),
        name="cache_layout",
    )(ck, cv)


def _rope_rows(x, cos, sin):
    q4 = HEAD_DIM // 4
    a1, a2, b1, b2 = (x[:, i * q4:(i + 1) * q4] for i in range(4))
    rot = jnp.concatenate([-a2, a1, -b2, b1], axis=1)
    return x * cos[None] + rot * sin[None]


def _head_rmsnorm_rows(x, gain):
    ms = jnp.mean(x * x, axis=1, keepdims=True)
    return x * lax.rsqrt(ms + EPS) * gain[None]


def _in_proj_kernel(*refs, use_rope, emit_kv):
    it = iter(refs)
    h_ref, mod_ref, g1_ref, wqkv_ref, wrest_ref, qg_ref, kg_ref = (next(it) for _ in range(7))
    cos_ref = sin_ref = None
    if use_rope:
        cos_ref, sin_ref = next(it), next(it)
    q_ref, k_ref, v_ref, a_ref, gate_ref = (next(it) for _ in range(5))
    nk_ref = nv_ref = None
    if emit_kv:
        nk_ref, nv_ref = next(it), next(it)

    tm = h_ref.shape[0]
    x = h_ref[...]
    mod = mod_ref[0]
    sh1 = mod[:, 0:D_MODEL]
    sc1 = mod[:, D_MODEL:2 * D_MODEL]
    ms = jnp.mean(x * x, axis=-1, keepdims=True)
    u = (x * lax.rsqrt(ms + EPS) * g1_ref[...]) * (1.0 + sc1) + sh1
    ub = u.astype(BF16)

    q_t = lax.dot_general(wqkv_ref[0:Q_W, :], ub, NT_DIMS, preferred_element_type=F32)
    q3 = _head_rmsnorm_rows(q_t.reshape(N_HEADS, HEAD_DIM, tm), qg_ref[...])
    if use_rope:
        q3 = _rope_rows(q3, cos_ref[...], sin_ref[...])
    q_ref[0] = q3.astype(BF16)

    k_t = lax.dot_general(wqkv_ref[Q_W:Q_W + KV_W, :], ub, NT_DIMS, preferred_element_type=F32)
    k3 = _head_rmsnorm_rows(k_t.reshape(N_KV_HEADS, HEAD_DIM, tm), kg_ref[...])
    if emit_kv:
        nk_ref[...] = k3.reshape(KV_W, tm).T
    if use_rope:
        k3 = _rope_rows(k3, cos_ref[...], sin_ref[...])
    k_tok = k3.reshape(KV_W, tm).T.astype(BF16)
    for hh in range(N_KV_HEADS):
        k_ref[0, hh] = k_tok[:, hh * HEAD_DIM:(hh + 1) * HEAD_DIM]

    v_t = lax.dot_general(wqkv_ref[Q_W + KV_W:QKV_W, :], ub, NT_DIMS, preferred_element_type=F32)
    v3 = v_t.reshape(N_KV_HEADS, HEAD_DIM, tm).astype(BF16)
    n_vb, _, v_lanes = v_ref.shape[2:]
    for j in range(n_vb):
        v_ref[0, :, j, 0:HEAD_DIM, :] = v3[:, :, j * v_lanes:(j + 1) * v_lanes]
        v_ref[0, :, j, HEAD_DIM:V_ROWS, :] = jnp.ones(
            (N_KV_HEADS, V_ROWS - HEAD_DIM, v_lanes), BF16)
    if emit_kv:
        nv_ref[...] = v_t.T

    glu = jnp.dot(ub, wrest_ref[:, 0:2 * D_CONV], preferred_element_type=F32)
    a_ref[0] = (glu[:, 0:D_CONV] * _sigmoid(glu[:, D_CONV:2 * D_CONV])).astype(BF16)
    gates = jnp.dot(ub, wrest_ref[:, 2 * D_CONV:], preferred_element_type=F32)
    gate_ref[...] = _sigmoid(gates).astype(BF16)


def _in_proj_call(h, mod, lw, rope, batch, seq, key_tile, emit_kv):
    tm = min(IN_PROJ_TILE, seq)
    nt = seq // tm
    n_key_blocks = seq // key_tile
    if tm >= key_tile:
        v_block = (1, N_KV_HEADS, tm // key_tile, V_ROWS, key_tile)
        v_index = lambda b, t: (b, 0, t, 0, 0)
    else:
        tiles_per_key = key_tile // tm
        v_block = (1, N_KV_HEADS, 1, V_ROWS, tm)
        v_index = lambda b, t: (b, 0, t // tiles_per_key, 0, t % tiles_per_key)
    use_rope = rope is not None
    tok = lambda b, t: (b * nt + t, 0)
    const2 = lambda b, t: (0, 0)
    in_specs = [
        pl.BlockSpec((tm, D_MODEL), tok),
        pl.BlockSpec((1, 1, N_MOD), lambda b, t: (b if mod.shape[0] > 1 else 0, 0, 0)),
        pl.BlockSpec((1, D_MODEL), const2),
        _resident((QKV_W, D_MODEL)),
        _resident((D_MODEL, 2 * D_CONV + 2 * D_MODEL)),
        pl.BlockSpec((HEAD_DIM, tm), const2),
        pl.BlockSpec((HEAD_DIM, tm), const2),
    ]
    args = [h, mod, lw["norm1_g"], lw["w_qkv_t"], lw["w_rest"],
            lw["q_gain"][:, :tm], lw["k_gain"][:, :tm]]
    if use_rope:
        in_specs += [pl.BlockSpec((HEAD_DIM, tm), lambda b, t: (0, t))] * 2
        args += list(rope)
    out_specs = [
        pl.BlockSpec((1, N_HEADS, HEAD_DIM, tm), lambda b, t: (b, 0, 0, t)),
        pl.BlockSpec((1, N_KV_HEADS, tm, HEAD_DIM), lambda b, t: (b, 0, t, 0)),
        pl.BlockSpec(v_block, v_index),
        pl.BlockSpec((1, tm, D_CONV), lambda b, t: (b, t, 0)),
        pl.BlockSpec((tm, 2 * D_MODEL), tok),
    ]
    out_shape = [
        jax.ShapeDtypeStruct((batch, N_HEADS, HEAD_DIM, seq), BF16),
        jax.ShapeDtypeStruct((batch, N_KV_HEADS, seq, HEAD_DIM), BF16),
        jax.ShapeDtypeStruct((batch, N_KV_HEADS, n_key_blocks, V_ROWS, key_tile), BF16),
        jax.ShapeDtypeStruct((batch, seq, D_CONV), BF16),
        jax.ShapeDtypeStruct((batch * seq, 2 * D_MODEL), BF16),
    ]
    if emit_kv:
        out_specs += [pl.BlockSpec((tm, KV_W), tok)] * 2
        out_shape += [jax.ShapeDtypeStruct((batch * seq, KV_W), F32)] * 2
    return pl.pallas_call(
        functools.partial(_in_proj_kernel, use_rope=use_rope, emit_kv=emit_kv),
        grid=(batch, nt),
        in_specs=in_specs,
        out_specs=out_specs,
        out_shape=out_shape,
        compiler_params=_params(),
        name="in_proj",
    )(*args)


def _attn_kernel(*refs, n_src):
    q_ref = refs[0]
    srcs = [(refs[1 + 2 * i], refs[2 + 2 * i]) for i in range(n_src)]
    o_ref = refs[1 + 2 * n_src]
    tq = q_ref.shape[3]

    q = jnp.concatenate([q_ref[0, j] for j in range(HEADS_PER_KV)], axis=1)
    m = acc = None
    for k_ref, v_ref in srcs:
        n_blocks, _, tk = v_ref.shape[2:]
        for j in range(n_blocks):
            kb = k_ref[0, 0, j * tk:(j + 1) * tk, :]
            vb = v_ref[0, 0, j]
            s = jnp.dot(kb, q, preferred_element_type=F32)
            block_max = jnp.max(s, axis=0, keepdims=True)
            m_new = block_max if m is None else jnp.maximum(m, block_max)
            p = jnp.exp2(s - m_new).astype(BF16)
            pv = jnp.dot(vb, p, preferred_element_type=F32)
            acc = pv if m is None else jnp.exp2(m - m_new) * acc + pv
            m = m_new

    out = acc[0:HEAD_DIM] / acc[HEAD_DIM:HEAD_DIM + 1]
    for j in range(HEADS_PER_KV):
        o_ref[0, j] = out[:, j * tq:(j + 1) * tq].astype(BF16)


def _attn_call(q, sources):
    batch, _, _, seq = q.shape
    tq = min(Q_TILE, seq)
    in_specs = [pl.BlockSpec((1, HEADS_PER_KV, HEAD_DIM, tq), lambda b, g, i: (b, g, 0, i))]
    args = [q]
    for k_all, v_all in sources:
        nk = k_all.shape[2]
        nb, _, tk = v_all.shape[2:]
        in_specs.append(pl.BlockSpec((1, 1, nk, HEAD_DIM), lambda b, g, i: (b, g, 0, 0)))
        in_specs.append(pl.BlockSpec((1, 1, nb, V_ROWS, tk), lambda b, g, i: (b, g, 0, 0, 0)))
        args += [k_all, v_all]
    return pl.pallas_call(
        functools.partial(_attn_kernel, n_src=len(sources)),
        grid=(batch, N_KV_HEADS, seq // tq),
        in_specs=in_specs,
        out_specs=pl.BlockSpec((1, HEADS_PER_KV, HEAD_DIM, tq), lambda b, g, i: (b, g, 0, i)),
        out_shape=jax.ShapeDtypeStruct((batch, N_HEADS, HEAD_DIM, seq), BF16),
        compiler_params=_params(),
        name="attn",
    )(*args)


def _router_rows(logits_ref, bias_ref, c_ref):
    lg = logits_ref[...]
    ex = jnp.exp(lg - jnp.max(lg, axis=0, keepdims=True))
    logits_ref[...] = ex / jnp.sum(ex, axis=0, keepdims=True)
    score = [logits_ref[e:e + 1, :] for e in range(N_EXPERTS)]
    sel = [score[e] + bias_ref[e:e + 1, :] for e in range(N_EXPERTS)]

    group_score = []
    for g in range(N_GROUPS):
        four = sel[g * EXPERTS_PER_GROUP:(g + 1) * EXPERTS_PER_GROUP]
        best = None
        for i in range(EXPERTS_PER_GROUP):
            for j in range(i + 1, EXPERTS_PER_GROUP):
                pair = four[i] + four[j]
                best = pair if best is None else jnp.maximum(best, pair)
        group_score.append(best)
    gmax = functools.reduce(jnp.maximum, group_score)

    in_group = []
    free = jnp.ones_like(gmax)
    for g in range(N_GROUPS):
        hit = jnp.where(group_score[g] == gmax, free, 0.0)
        in_group.append(hit)
        free = free - hit

    def pick(values, i):
        out = jnp.zeros_like(gmax)
        for g in range(N_GROUPS):
            out = jnp.where(in_group[g] > 0.0, values[g * EXPERTS_PER_GROUP + i], out)
        return out

    cand_sel = [pick(sel, i) for i in range(EXPERTS_PER_GROUP)]
    cand_score = [pick(score, i) for i in range(EXPERTS_PER_GROUP)]
    chosen = []
    for i in range(EXPERTS_PER_GROUP):
        rank = jnp.zeros_like(gmax)
        for j in range(EXPERTS_PER_GROUP):
            if j < i:
                rank = rank + jnp.where(cand_sel[j] >= cand_sel[i], 1.0, 0.0)
            elif j > i:
                rank = rank + jnp.where(cand_sel[j] > cand_sel[i], 1.0, 0.0)
        chosen.append(jnp.where(rank < 2.0, cand_score[i], 0.0))
    inv = 1.0 / functools.reduce(lambda a, b: a + b, chosen)
    weight = [c * inv for c in chosen]
    for g in range(N_GROUPS):
        for i in range(EXPERTS_PER_GROUP):
            e = g * EXPERTS_PER_GROUP + i
            c_ref[e:e + 1, :] = in_group[g] * weight[i]


def _post_kernel(*refs, has_halo):
    it = iter(refs)
    h_ref, o_ref, a_ref = next(it), next(it), next(it)
    ap_ref = an_ref = None
    if has_halo:
        ap_ref, an_ref = next(it), next(it)
    (gate_ref, mod_ref, wao_ref, wco_ref, wout_ref, cw_ref, cb_ref, lg_ref, lb_ref) = (
        next(it) for _ in range(9))
    hout_ref = next(it)
    xs_ref = next(it)

    tm = h_ref.shape[0]
    g1 = mod_ref[0][:, 2 * D_MODEL:3 * D_MODEL]

    o_t = o_ref[0].reshape(Q_W, tm)
    attn_br = lax.dot_general(o_t, wao_ref[...], TN_DIMS, preferred_element_type=F32)

    zeros_halo = jnp.zeros((CONV_HALO, D_CONV), F32)
    if has_halo:
        t = pl.program_id(1)
        last = pl.num_programs(1) - 1
        before = jnp.where(t > 0, ap_ref[0].astype(F32), zeros_halo)
        after = jnp.where(t < last, an_ref[0].astype(F32), zeros_halo)
    else:
        before = after = zeros_halo
    a_cur = a_ref[0].astype(F32)
    slabs = []
    for s in range(D_CONV // LANES):
        cols = slice(s * LANES, (s + 1) * LANES)
        xs_ref[s, 0:CONV_HALO, :] = before[:, cols]
        xs_ref[s, CONV_HALO:CONV_HALO + tm, :] = a_cur[:, cols]
        xs_ref[s, CONV_HALO + tm:, :] = after[:, cols]
        acc = jnp.zeros((tm, LANES), F32) + cb_ref[:, cols]
        for j in range(CONV_K):
            start = CONV_HALO - CONV_PAD + j
            acc = acc + xs_ref[s, start:start + tm, :] * cw_ref[j:j + 1, cols]
        slabs.append(acc)
    cv = jnp.concatenate(slabs, axis=1)
    mu = jnp.mean(cv, axis=-1, keepdims=True)
    dev = cv - mu
    var = jnp.mean(dev * dev, axis=-1, keepdims=True)
    y = dev * lax.rsqrt(var + EPS) * lg_ref[...] + lb_ref[...]
    y = y * _sigmoid(y)
    conv_br = jnp.dot(y.astype(BF16), wco_ref[...], preferred_element_type=F32)

    gates = gate_ref[...].astype(F32)
    merged = gates[:, 0:D_MODEL] * attn_br + gates[:, D_MODEL:] * conv_br
    out = jnp.dot(merged.astype(BF16), wout_ref[...], preferred_element_type=F32)
    hout_ref[...] = h_ref[...] + g1 * out


def _post_call(h, o_t, a, gates, mod, lw, batch, seq):
    tm = min(TOKEN_TILE, seq)
    nt = seq // tm
    has_halo = nt > 1
    halo_blocks_per_tile = tm // CONV_HALO
    n_halo_blocks = seq // CONV_HALO
    tok = lambda b, t: (b * nt + t, 0)
    const2 = lambda b, t: (0, 0)
    in_specs = [
        pl.BlockSpec((tm, D_MODEL), tok),
        pl.BlockSpec((1, N_HEADS, HEAD_DIM, tm), lambda b, t: (b, 0, 0, t)),
        pl.BlockSpec((1, tm, D_CONV), lambda b, t: (b, t, 0)),
    ]
    args = [h, o_t, a]
    if has_halo:
        in_specs += [
            pl.BlockSpec((1, CONV_HALO, D_CONV),
                         lambda b, t: (b, jnp.maximum(t * halo_blocks_per_tile - 1, 0), 0)),
            pl.BlockSpec((1, CONV_HALO, D_CONV),
                         lambda b, t: (b, jnp.minimum((t + 1) * halo_blocks_per_tile,
                                                      n_halo_blocks - 1), 0)),
        ]
        args += [a, a]
    in_specs += [
        pl.BlockSpec((tm, 2 * D_MODEL), tok),
        pl.BlockSpec((1, 1, N_MOD), lambda b, t: (b if mod.shape[0] > 1 else 0, 0, 0)),
        _resident((Q_W, D_MODEL)),
        _resident((D_CONV, D_MODEL)),
        _resident((D_MODEL, D_MODEL)),
        pl.BlockSpec((CONV_K, D_CONV), const2),
        pl.BlockSpec((1, D_CONV), const2),
        pl.BlockSpec((1, D_CONV), const2),
        pl.BlockSpec((1, D_CONV), const2),
    ]
    args += [gates, mod, lw["w_attn_o"], lw["w_conv_o"], lw["w_out"], lw["conv_w"], lw["conv_b"],
             lw["conv_ln_g"], lw["conv_ln_b"]]
    return pl.pallas_call(
        functools.partial(_post_kernel, has_halo=has_halo),
        grid=(batch, nt),
        in_specs=in_specs,
        out_specs=pl.BlockSpec((tm, D_MODEL), tok),
        out_shape=jax.ShapeDtypeStruct((batch * seq, D_MODEL), F32),
        scratch_shapes=[pltpu.VMEM((D_CONV // LANES, tm + 2 * CONV_HALO, LANES), F32)],
        compiler_params=_params(),
        name="post",
    )(*args)


MOE_ROW_CHUNK = 1024


def _moe_kernel(h_ref, mod_ref, g2_ref, wrh_ref, wrl_ref, rb_ref, wg_ref, wu_ref, wd_ref, fg_ref,
                o_ref, mid_ref, lgt_ref, c_ref, *, final_norm):
    h1 = h_ref[...]
    tm = h1.shape[0]
    mod = mod_ref[0]
    sh2 = mod[:, 3 * D_MODEL:4 * D_MODEL]
    sc2 = mod[:, 4 * D_MODEL:5 * D_MODEL]
    ms = jnp.mean(h1 * h1, axis=-1, keepdims=True)
    u2 = (h1 * lax.rsqrt(ms + EPS) * g2_ref[...]) * (1.0 + sc2) + sh2
    u = u2.astype(BF16)
    u_lo = (u2 - u.astype(F32)).astype(BF16)
    lgt_ref[...] = (
        lax.dot_general(wrh_ref[...], u, NT_DIMS, preferred_element_type=F32)
        + lax.dot_general(wrh_ref[...], u_lo, NT_DIMS, preferred_element_type=F32)
        + lax.dot_general(wrl_ref[...], u, NT_DIMS, preferred_element_type=F32))
    _router_rows(lgt_ref, rb_ref, c_ref)

    experts_per_chunk = MOE_ROW_CHUNK // D_EXPERT
    for r in range(MOE_W // MOE_ROW_CHUNK):
        rows = slice(r * MOE_ROW_CHUNK, (r + 1) * MOE_ROW_CHUNK)
        hg = lax.dot_general(wg_ref[rows, :], u, NT_DIMS, preferred_element_type=F32)
        hu = lax.dot_general(wu_ref[rows, :], u, NT_DIMS, preferred_element_type=F32)
        act = (hg * _sigmoid(hg)) * hu
        act = act.reshape(experts_per_chunk, D_EXPERT, tm)
        c = c_ref[r * experts_per_chunk:(r + 1) * experts_per_chunk, :]
        mid_ref[rows, :] = (act * c[:, None, :]).reshape(MOE_ROW_CHUNK, tm).astype(BF16)
    y_t = jnp.dot(wd_ref[...], mid_ref[...], preferred_element_type=F32)
    g2 = mod[:, 5 * D_MODEL:6 * D_MODEL]
    h2 = h1 + g2 * y_t.T
    if final_norm:
        ms2 = jnp.mean(h2 * h2, axis=-1, keepdims=True)
        h2 = h2 * lax.rsqrt(ms2 + EPS) * fg_ref[...]
    o_ref[...] = h2


def _moe_call(h, mod, lw, shared, batch, seq, final_norm):
    tm = min(TOKEN_TILE, seq)
    nt = seq // tm
    tok = lambda b, t: (b * nt + t, 0)
    const2 = lambda b, t: (0, 0)
    return pl.pallas_call(
        functools.partial(_moe_kernel, final_norm=final_norm),
        grid=(batch, nt),
        in_specs=[
            pl.BlockSpec((tm, D_MODEL), tok),
            pl.BlockSpec((1, 1, N_MOD), lambda b, t: (b if mod.shape[0] > 1 else 0, 0, 0)),
            pl.BlockSpec((1, D_MODEL), const2),
            pl.BlockSpec((N_EXPERTS, D_MODEL), const2),
            pl.BlockSpec((N_EXPERTS, D_MODEL), const2),
            pl.BlockSpec((N_EXPERTS, tm), const2),
            _resident((MOE_W, D_MODEL)),
            _resident((MOE_W, D_MODEL)),
            _resident((D_MODEL, MOE_W)),
            pl.BlockSpec((1, D_MODEL), const2),
        ],
        out_specs=pl.BlockSpec((tm, D_MODEL), tok),
        out_shape=jax.ShapeDtypeStruct((batch * seq, D_MODEL), F32),
        scratch_shapes=[
            pltpu.VMEM((MOE_W, tm), BF16),
            pltpu.VMEM((N_EXPERTS, tm), F32),
            pltpu.VMEM((N_EXPERTS, tm), F32),
        ],
        compiler_params=_params(),
        name="moe",
    )(h, mod, lw["norm2_g"], shared["w_router_hi"], shared["w_router_lo"],
      shared["router_bias"][:, :tm], lw["w_gate_t"], lw["w_up_t"], lw["w_down_t"],
      shared["final_gain"])


def _rope_tables_t(n_tokens):
    axis_dim = HEAD_DIM // 2
    rows = n_tokens // GRID_W
    row = np.repeat(np.arange(rows), GRID_W).astype(np.float64)
    col = np.tile(np.arange(GRID_W), rows).astype(np.float64)
    inv = 1.0 / (ROPE_THETA ** (np.arange(0, axis_dim, 2, dtype=np.float64) / axis_dim))
    ar = inv[:, None] * row[None, :]
    ac = inv[:, None] * col[None, :]
    cos = np.concatenate([np.cos(ar), np.cos(ar), np.cos(ac), np.cos(ac)], axis=0)
    sin = np.concatenate([np.sin(ar), np.sin(ar), np.sin(ac), np.sin(ac)], axis=0)
    return jnp.asarray(cos, F32), jnp.asarray(sin, F32)


def _layer_weights(l, w):
    tm = IN_PROJ_TILE
    scale = math.log2(math.e) / math.sqrt(HEAD_DIM)
    w_in = w["w_in"][l]
    return {
        "norm1_g": w["norm1_g"][l].reshape(1, D_MODEL),
        "w_qkv_t": w_in[:, :QKV_W].T.astype(BF16),
        "w_rest": w_in[:, QKV_W:].astype(BF16),
        "q_gain": jnp.broadcast_to((w["q_norm_g"][l] * scale)[:, None], (HEAD_DIM, tm)),
        "k_gain": jnp.broadcast_to(w["k_norm_g"][l][:, None], (HEAD_DIM, tm)),
        "w_attn_o": w["w_attn_o"][l].astype(BF16),
        "w_conv_o": w["w_conv_o"][l].astype(BF16),
        "w_out": w["w_out"][l].astype(BF16),
        "conv_w": w["conv_w"][l],
        "conv_b": w["conv_b"][l].reshape(1, D_CONV),
        "conv_ln_g": w["conv_ln_g"][l].reshape(1, D_CONV),
        "conv_ln_b": w["conv_ln_b"][l].reshape(1, D_CONV),
        "norm2_g": w["norm2_g"][l].reshape(1, D_MODEL),
        "w_gate_t": w["w_gate"][l].transpose(0, 2, 1).reshape(MOE_W, D_MODEL).astype(BF16),
        "w_up_t": w["w_up"][l].transpose(0, 2, 1).reshape(MOE_W, D_MODEL).astype(BF16),
        "w_down_t": w["w_down"][l].transpose(2, 0, 1).reshape(D_MODEL, MOE_W).astype(BF16),
    }


def _trunk_layer(h, mod, lw, shared, rope, cache, batch, seq, emit_kv, final_norm):
    key_tile = min(KEY_TILE_MAX, seq)
    outs = _in_proj_call(h, mod, lw, rope, batch, seq, key_tile, emit_kv)
    q, k_att, v_att, a, gates = outs[:5]
    sources = ([cache] if cache is not None else []) + [(k_att, v_att)]
    o_t = _attn_call(q, sources)
    h1 = _post_call(h, o_t, a, gates, mod, lw, batch, seq)
    h2 = _moe_call(h1, mod, lw, shared, batch, seq, final_norm)
    return h2, outs[5:]


def kernel(x_prompt, x_sample, cache_k, cache_v, c, c_ctx, norm1_g, w_ada, b_ada, w_in, q_norm_g,
           k_norm_g, w_attn_o, conv_w, conv_b, conv_ln_g, conv_ln_b, w_conv_o, w_out, norm2_g,
           w_router, router_bias, w_gate, w_up, w_down, final_norm_g):
    n_ctx, ctx_len, _ = x_prompt.shape
    n_dec, dec_len, _ = x_sample.shape
    depth = w_in.shape[0]
    weights = dict(norm1_g=norm1_g, w_in=w_in, q_norm_g=q_norm_g, k_norm_g=k_norm_g,
                   w_attn_o=w_attn_o, conv_w=conv_w, conv_b=conv_b, conv_ln_g=conv_ln_g,
                   conv_ln_b=conv_ln_b, w_conv_o=w_conv_o, w_out=w_out, norm2_g=norm2_g,
                   w_gate=w_gate, w_up=w_up, w_down=w_down)

    cvecs = jnp.zeros((ADA_ROWS, D_MODEL), F32).at[0].set(c_ctx).at[1:1 + n_dec].set(c)
    mod_all = _ada_call(cvecs, w_ada, b_ada)
    cache_kp, cache_vp = _cache_call(cache_k, cache_v, min(KEY_TILE_MAX, dec_len))
    rope = _rope_tables_t(dec_len)

    w_router_t = w_router.T
    w_router_hi = w_router_t.astype(BF16)
    shared = {
        "w_router_hi": w_router_hi,
        "w_router_lo": (w_router_t - w_router_hi.astype(F32)).astype(BF16),
        "router_bias": jnp.broadcast_to(router_bias[:, None], (N_EXPERTS, TOKEN_TILE)),
        "final_gain": final_norm_g.reshape(1, D_MODEL),
    }

    hp = x_prompt.reshape(n_ctx * ctx_len, D_MODEL)
    hs = x_sample.reshape(n_dec * dec_len, D_MODEL)
    new_k, new_v = [], []
    for l in range(depth):
        lw = _layer_weights(l, weights)
        last = l == depth - 1
        mod_p = mod_all[l, 0:1].reshape(1, 1, N_MOD)
        mod_s = mod_all[l, 1:1 + n_dec].reshape(n_dec, 1, N_MOD)
        hp, (nk, nv) = _trunk_layer(hp, mod_p, lw, shared, None, None, n_ctx, ctx_len, True, last)
        new_k.append(nk.reshape(n_ctx, ctx_len, N_KV_HEADS, HEAD_DIM))
        new_v.append(nv.reshape(n_ctx, ctx_len, N_KV_HEADS, HEAD_DIM))
        cache = (cache_kp[:, l], cache_vp[:, l])
        hs, _ = _trunk_layer(hs, mod_s, lw, shared, rope, cache, n_dec, dec_len, False, last)

    y_prompt = hp.reshape(n_ctx, ctx_len, D_MODEL)
    y_sample = hs.reshape(n_dec, dec_len, D_MODEL)
    return (y_prompt, y_sample, jnp.stack(new_k, axis=1), jnp.stack(new_v, axis=1))
```
